```python
import math
import jax
import jax.numpy as jnp
from jax import lax
import numpy as np

D_MODEL = 2048
BATCH = 4
SEQ = 4096
DEPTH = 2

GRID_W = 64
CTX_LEN = 256
N_MOD = 6
EPS = 1e-6
F32 = jnp.float32

ATTN_HEADS = 8
ATTN_KV_HEADS = 2
HEAD_DIM = 128
ATTN_GROUP = ATTN_HEADS // ATTN_KV_HEADS
ATTN_SCALE = HEAD_DIM ** -0.5
WINDOW = 128
BLOCK = 128
ROPE_THETA = 10000.0

S5_WIDTH = D_MODEL // 2
S5_CH = 16
S5_GROUPS = S5_WIDTH // S5_CH
S5_STATE = 64

AB_Q = ATTN_HEADS * HEAD_DIM
AB_KV = ATTN_KV_HEADS * HEAD_DIM
AB_IN = AB_Q + 2 * AB_KV + S5_WIDTH
AB_OUT = AB_Q + S5_WIDTH

SSD_INNER = 2 * D_MODEL
SSD_HEAD_DIM = 64
SSD_HEADS = SSD_INNER // SSD_HEAD_DIM
SSD_GROUPS = 8
SSD_HPG = SSD_HEADS // SSD_GROUPS
SSD_STATE = 128
SSD_CONV = 5
SSD_CHUNK = 128
SSD_XBC = SSD_INNER + 2 * SSD_GROUPS * SSD_STATE
SSD_IN = SSD_INNER + SSD_XBC + 2 * SSD_HEADS

PEER_HEADS = 8
PEER_NKEYS = 128
PEER_EXPERTS = PEER_NKEYS * PEER_NKEYS
PEER_QDIM = 128
PEER_HALF = PEER_QDIM // 2
PEER_TOPK = 16
PEER_TOKENS = 128

kernel_name = "hybrid_swa_s5_ssd_peer_dit"


def rmsnorm(x, g):
    x32 = x.astype(F32)
    y = x32 * lax.rsqrt(jnp.mean(x32 * x32, axis=-1, keepdims=True) + EPS)
    return (y * g.astype(F32)).astype(x.dtype)


def modulate(h, shift, scale):
    return h * (1 + scale) + shift


def axial_rope(x, length):
    t = jnp.arange(length)
    row = (t // GRID_W).astype(F32)
    col = (t % GRID_W).astype(F32)
    n_freq = HEAD_DIM // 4
    inv = ROPE_THETA ** (-jnp.arange(n_freq, dtype=F32) / n_freq)
    ang = jnp.concatenate([row[:, None] * inv, col[:, None] * inv], axis=-1)
    ang = ang.reshape((1, length) + (1,) * (x.ndim - 3) + (HEAD_DIM // 2,))
    cos = jnp.cos(ang).astype(x.dtype)
    sin = jnp.sin(ang).astype(x.dtype)
    x1, x2 = jnp.split(x, 2, axis=-1)
    return jnp.concatenate([x1 * cos - x2 * sin, x2 * cos + x1 * sin], axis=-1)


def window_attention_latent(q, k, v, k_ctx, v_ctx, sink):
    bsz, seq = q.shape[:2]
    nb = seq // BLOCK
    qb = q.reshape(bsz, nb, BLOCK, ATTN_KV_HEADS, ATTN_GROUP, HEAD_DIM)

    def bands(t):
        tp = jnp.pad(t, ((0, 0), (BLOCK, BLOCK), (0, 0), (0, 0)))
        tp = tp.reshape(bsz, nb + 2, BLOCK, ATTN_KV_HEADS, HEAD_DIM)
        return jnp.concatenate([tp[:, :-2], tp[:, 1:-1], tp[:, 2:]], axis=2)

    kb, vb = bands(k), bands(v)
    s_band = jnp.einsum('bnqhgd,bnkhd->bnhgqk', qb, kb).astype(F32) * ATTN_SCALE
    start = jnp.arange(nb)[:, None, None] * BLOCK
    qpos = start + jnp.arange(BLOCK)[None, :, None]
    kpos = start - BLOCK + jnp.arange(3 * BLOCK)[None, None, :]
    ok = (jnp.abs(qpos - kpos) <= WINDOW) & (kpos >= 0) & (kpos < seq)
    s_band = jnp.where(ok[None, :, None, None], s_band, -jnp.inf)
    s_ctx = jnp.einsum('bnqhgd,bchd->bnhgqc', qb, k_ctx).astype(F32) * ATTN_SCALE
    s_sink = jnp.broadcast_to(sink.astype(F32)[None, None, :, :, None, None], s_ctx.shape[:-1] + (1,))
    p = jax.nn.softmax(jnp.concatenate([s_sink, s_ctx, s_band], axis=-1), axis=-1).astype(v.dtype)
    n_ctx = k_ctx.shape[1]
    o = (jnp.einsum('bnhgqc,bchd->bnqhgd', p[..., 1:1 + n_ctx], v_ctx)
         + jnp.einsum('bnhgqk,bnkhd->bnqhgd', p[..., 1 + n_ctx:], vb))
    return o.reshape(bsz, seq, AB_Q)


def context_attention(q, k, v, sink):
    bsz, length = q.shape[:2]
    s = jnp.einsum('bqhgd,bkhd->bhgqk', q, k).astype(F32) * ATTN_SCALE
    s_sink = jnp.broadcast_to(sink.astype(F32)[None, :, :, None, None], s.shape[:-1] + (1,))
    p = jax.nn.softmax(jnp.concatenate([s_sink, s], axis=-1), axis=-1).astype(v.dtype)
    o = jnp.einsum('bhgqk,bkhd->bqhgd', p[..., 1:], v)
    return o.reshape(bsz, length, AB_Q)


def s5_discretise(a_re, a_im, log_dt, b_re, b_im):
    dt = jnp.exp(log_dt.astype(F32))
    ar, ai = a_re.astype(F32), a_im.astype(F32)
    mag = jnp.exp(dt * ar)
    abar_re, abar_im = mag * jnp.cos(dt * ai), mag * jnp.sin(dt * ai)
    den = ar * ar + ai * ai
    f_re = ((abar_re - 1) * ar + abar_im * ai) / den
    f_im = (abar_im * ar - (abar_re - 1) * ai) / den
    br, bi = b_re.astype(F32), b_im.astype(F32)
    bbar_re = f_re[..., None] * br - f_im[..., None] * bi
    bbar_im = f_re[..., None] * bi + f_im[..., None] * br
    return abar_re, abar_im, bbar_re, bbar_im


def _complex_affine_combine(e1, e2):
    a1r, a1i, b1r, b1i = e1
    a2r, a2i, b2r, b2i = e2
    return (a2r * a1r - a2i * a1i, a2r * a1i + a2i * a1r,
            a2r * b1r - a2i * b1i + b2r, a2r * b1i + a2i * b1r + b2i)


def complex_diag_scan(abar_re, abar_im, bu_re, bu_im, reverse, init=None):
    if init is not None:
        s_re, s_im = init
        first = -1 if reverse else 0
        bu_re = bu_re.at[:, first].add(abar_re * s_re - abar_im * s_im)
        bu_im = bu_im.at[:, first].add(abar_re * s_im + abar_im * s_re)
    length = bu_re.shape[1]
    ar = jnp.broadcast_to(abar_re, (1, length) + abar_re.shape)
    ai = jnp.broadcast_to(abar_im, (1, length) + abar_im.shape)
    _, _, st_re, st_im = lax.associative_scan(_complex_affine_combine, (ar, ai, bu_re, bu_im),
                                              reverse=reverse, axis=1)
    return st_re, st_im


def s5_mixer(u_lat, u_ctx, a_re, a_im, log_dt, b_re, b_im, c_re, c_im, d, glu_w, glu_b, with_ctx):
    def grp(u):
        return u.reshape(u.shape[0], u.shape[1], S5_GROUPS, S5_CH).astype(F32)

    ul, uc = grp(u_lat), grp(u_ctx)
    dd = d.astype(F32)
    y_lat = ul * dd
    y_ctx = uc * dd if with_ctx else None
    for direction in range(2):
        reverse = direction == 1
        abr, abi, bbr, bbi = s5_discretise(a_re[direction], a_im[direction], log_dt[direction],
                                           b_re[direction], b_im[direction])
        cr, ci = c_re[direction].astype(F32), c_im[direction].astype(F32)

        def drive(u):
            return (jnp.einsum('blgc,gpc->blgp', u, bbr), jnp.einsum('blgc,gpc->blgp', u, bbi))

        def read(sr, si):
            return jnp.einsum('blgp,gcp->blgc', sr, cr) - jnp.einsum('blgp,gcp->blgc', si, ci)

        cs_re, cs_im = complex_diag_scan(abr, abi, *drive(uc), reverse)
        end = 0 if reverse else -1
        ls_re, ls_im = complex_diag_scan(abr, abi, *drive(ul), reverse,
                                         init=(cs_re[:, end], cs_im[:, end]))
        y_lat = y_lat + read(ls_re, ls_im)
        if with_ctx:
            y_ctx = y_ctx + read(cs_re, cs_im)

    def glu(y, dtype):
        g = jax.nn.gelu(y.reshape(y.shape[0], y.shape[1], S5_WIDTH), approximate=False)
        return (g * jax.nn.sigmoid(g @ glu_w.astype(F32) + glu_b.astype(F32))).astype(dtype)

    return glu(y_lat, u_lat.dtype), (glu(y_ctx, u_ctx.dtype) if with_ctx else None)


def attn_s5_mixer(h_lat, h_ctx, w_in, sink, a_re, a_im, log_dt, b_re, b_im, c_re, c_im, d,
                  glu_w, glu_b, w_out, with_ctx):
    def split(p):
        bsz, length = p.shape[:2]
        q, k, v, u = jnp.split(p, [AB_Q, AB_Q + AB_KV, AB_Q + 2 * AB_KV], axis=-1)
        return (q.reshape(bsz, length, ATTN_KV_HEADS, ATTN_GROUP, HEAD_DIM),
                k.reshape(bsz, length, ATTN_KV_HEADS, HEAD_DIM),
                v.reshape(bsz, length, ATTN_KV_HEADS, HEAD_DIM), u)

    ql, kl, vl, ul = split(h_lat @ w_in)
    qc, kc, vc, uc = split(h_ctx @ w_in)
    seq = h_lat.shape[1]
    ql, kl = axial_rope(ql, seq), axial_rope(kl, seq)
    sink_g = sink.reshape(ATTN_KV_HEADS, ATTN_GROUP)
    attn_lat = window_attention_latent(ql, kl, vl, kc, vc, sink_g)
    s5_lat, s5_ctx = s5_mixer(ul, uc, a_re, a_im, log_dt, b_re, b_im, c_re, c_im, d, glu_w, glu_b, with_ctx)
    out_lat = jnp.concatenate([attn_lat, s5_lat], axis=-1) @ w_out
    out_ctx = None
    if with_ctx:
        attn_ctx = context_attention(qc, kc, vc, sink_g)
        out_ctx = jnp.concatenate([attn_ctx, s5_ctx], axis=-1) @ w_out
    return out_lat, out_ctx


def centred_depthwise_conv(x, w, b):
    y = lax.conv_general_dilated(x, w[:, None, :].astype(x.dtype), window_strides=(1,),
                                 padding=[(SSD_CONV // 2, SSD_CONV // 2)],
                                 dimension_numbers=('NWC', 'WIO', 'NWC'),
                                 feature_group_count=x.shape[-1])
    return y + b.astype(x.dtype)


def ssd_inputs(h, w_in, conv_w, conv_b, dt_bias):
    bsz, length = h.shape[:2]
    p = h @ w_in
    z, xbc, dt_raw = jnp.split(p, [SSD_INNER, SSD_INNER + SSD_XBC], axis=-1)
    xbc = jax.nn.silu(centred_depthwise_conv(xbc, conv_w, conv_b))
    xs, bm, cm = jnp.split(xbc, [SSD_INNER, SSD_INNER + SSD_GROUPS * SSD_STATE], axis=-1)
    dt = jax.nn.softplus((dt_raw.reshape(bsz, length, 2, SSD_HEADS) + dt_bias).astype(F32))
    return (z, xs.reshape(bsz, length, SSD_HEADS, SSD_HEAD_DIM),
            bm.reshape(bsz, length, SSD_GROUPS, SSD_STATE),
            cm.reshape(bsz, length, SSD_GROUPS, SSD_STATE), dt)


def ssd_chunked(x, dt, a, bm, cm, init):
    bsz, length = x.shape[:2]
    nc = length // SSD_CHUNK
    xd = (x.astype(F32) * dt[..., None]).reshape(bsz, nc, SSD_CHUNK, SSD_GROUPS, SSD_HPG, SSD_HEAD_DIM)
    la = (dt * a).reshape(bsz, nc, SSD_CHUNK, SSD_GROUPS, SSD_HPG).transpose(0, 3, 4, 1, 2)
    cs = jnp.cumsum(la, axis=-1)
    bc = bm.reshape(bsz, nc, SSD_CHUNK, SSD_GROUPS, SSD_STATE).astype(F32)
    cc = cm.reshape(bsz, nc, SSD_CHUNK, SSD_GROUPS, SSD_STATE).astype(F32)
    tril = jnp.tril(jnp.ones((SSD_CHUNK, SSD_CHUNK), dtype=bool))
    decay_in = jnp.exp(jnp.where(tril, cs[..., :, None] - cs[..., None, :], -jnp.inf))
    cb = jnp.einsum('bclgn,bcsgn->bcgls', cc, bc)
    y_diag = jnp.einsum('bcgls,bgjcls,bcsgjp->bclgjp', cb, decay_in, xd)
    decay_to_end = jnp.exp(cs[..., -1:] - cs)
    states = jnp.einsum('bclgn,bgjcl,bclgjp->bcgjpn', bc, decay_to_end, xd)
    init_g = init.reshape(bsz, 1, SSD_GROUPS, SSD_HPG, SSD_HEAD_DIM, SSD_STATE).astype(F32)
    states = jnp.concatenate([init_g, states], axis=1)
    tot = jnp.cumsum(jnp.pad(cs[..., -1], ((0, 0), (0, 0), (0, 0), (1, 0))), axis=-1)
    tril_c = jnp.tril(jnp.ones((nc + 1, nc + 1), dtype=bool))
    decay_chunk = jnp.exp(jnp.where(tril_c, tot[..., :, None] - tot[..., None, :], -jnp.inf))
    carried = jnp.einsum('bgjzc,bcgjpn->bzgjpn', decay_chunk, states)
    y_off = jnp.einsum('bclgn,bcgjpn,bgjcl->bclgjp', cc, carried[:, :-1], jnp.exp(cs))
    y = (y_diag + y_off).reshape(bsz, length, SSD_HEADS, SSD_HEAD_DIM)
    return y, carried[:, -1].reshape(bsz, SSD_HEADS, SSD_HEAD_DIM, SSD_STATE)


def ssd_final_state(x, dt, a, bm):
    bsz, length = x.shape[:2]
    cs = jnp.cumsum(dt * a, axis=1)
    w = (jnp.exp(cs[:, -1:] - cs) * dt).reshape(bsz, length, SSD_GROUPS, SSD_HPG)
    xg = x.astype(F32).reshape(bsz, length, SSD_GROUPS, SSD_HPG, SSD_HEAD_DIM)
    st = jnp.einsum('blgn,blgj,blgjp->bgjpn', bm.astype(F32), w, xg)
    return st.reshape(bsz, SSD_HEADS, SSD_HEAD_DIM, SSD_STATE)


def ssd_mixer(h_lat, h_ctx, w_in, conv_w, conv_b, dt_bias, a_log, d, norm_g, w_out, with_ctx):
    zl, xl, bl, cl, dtl = ssd_inputs(h_lat, w_in, conv_w, conv_b, dt_bias)
    zc, xc, bcx, ccx, dtc = ssd_inputs(h_ctx, w_in, conv_w, conv_b, dt_bias)
    a = -jnp.exp(a_log.astype(F32))
    dskip = d.astype(F32)[:, None]
    y_lat = xl.astype(F32) * dskip
    y_ctx = xc.astype(F32) * dskip if with_ctx else None
    for direction in range(2):
        if direction == 0:
            prep = lambda t: t
        else:
            prep = lambda t: jnp.flip(t, axis=1)
        xc_d, bc_d, cc_d, dtc_d = prep(xc), prep(bcx), prep(ccx), prep(dtc[:, :, direction])
        if with_ctx:
            zero = jnp.zeros((xc.shape[0], SSD_HEADS, SSD_HEAD_DIM, SSD_STATE), F32)
            yc_d, ctx_state = ssd_chunked(xc_d, dtc_d, a[direction], bc_d, cc_d, zero)
            y_ctx = y_ctx + prep(yc_d)
        else:
            ctx_state = ssd_final_state(xc_d, dtc_d, a[direction], bc_d)
        yl_d, _ = ssd_chunked(prep(xl), prep(dtl[:, :, direction]), a[direction], prep(bl), prep(cl), ctx_state)
        y_lat = y_lat + prep(yl_d)

    def finish(y, z, dtype):
        bsz, length = y.shape[:2]
        g = y.reshape(bsz, length, SSD_INNER) * jax.nn.silu(z.astype(F32))
        g = rmsnorm(g.reshape(bsz, length, SSD_GROUPS, SSD_INNER // SSD_GROUPS),
                    norm_g.reshape(SSD_GROUPS, SSD_INNER // SSD_GROUPS))
        return g.reshape(bsz, length, SSD_INNER).astype(dtype) @ w_out

    return finish(y_lat, zl, h_lat.dtype), (finish(y_ctx, zc, h_ctx.dtype) if with_ctx else None)


def peer_ffn(h, wq, keys, u, v):
    shape = h.shape
    hs = h.reshape(-1, PEER_TOKENS, shape[-1])

    def one_block(hb):
        q = (hb @ wq).reshape(hb.shape[0], PEER_HEADS, 2, PEER_HALF).astype(F32)
        s1 = jnp.einsum('thd,hkd->thk', q[:, :, 0], keys[:, 0].astype(F32))
        s2 = jnp.einsum('thd,hkd->thk', q[:, :, 1], keys[:, 1].astype(F32))
        v1, i1 = lax.top_k(s1, PEER_TOPK)
        v2, i2 = lax.top_k(s2, PEER_TOPK)
        cand = (v1[..., :, None] + v2[..., None, :]).reshape(hb.shape[0], PEER_HEADS, PEER_TOPK * PEER_TOPK)
        score, flat = lax.top_k(cand, PEER_TOPK)
        e1 = jnp.take_along_axis(i1, flat // PEER_TOPK, axis=-1)
        e2 = jnp.take_along_axis(i2, flat % PEER_TOPK, axis=-1)
        expert = e1 * PEER_NKEYS + e2
        gate = jax.nn.softmax(score, axis=-1)
        act = jax.nn.gelu(jnp.einsum('thkd,td->thk', u[expert], hb).astype(F32), approximate=False)
        return jnp.einsum('thk,thkd->td', (gate * act).astype(hb.dtype), v[expert])

    return lax.map(one_block, hs).reshape(shape)


def setup_inputs(seed: int = 0) -> dict:
    key = jax.random.key(seed)
    ks = iter(jax.random.split(key, 48))

    def nrm(shape, s):
        return jax.random.normal(next(ks), shape, F32) * s

    n_even = (DEPTH + 1) // 2
    n_odd = DEPTH // 2
    x = nrm((BATCH, SEQ, D_MODEL), 1.0)
    c = nrm((BATCH, D_MODEL), 1.0)
    ctx = nrm((BATCH, CTX_LEN, D_MODEL), 1.0)
    c_ctx = nrm((D_MODEL,), 1.0)
    mod_w = nrm((DEPTH, D_MODEL, N_MOD * D_MODEL), 0.5 * D_MODEL ** -0.5)
    mod_b = nrm((DEPTH, N_MOD * D_MODEL), 0.02)
    norm1_g = 1.0 + nrm((DEPTH, D_MODEL), 0.02)
    norm2_g = 1.0 + nrm((DEPTH, D_MODEL), 0.02)
    ab_w_in = nrm((n_even, D_MODEL, AB_IN), D_MODEL ** -0.5)
    attn_sink = nrm((n_even, ATTN_HEADS), 0.5)
    s5_shape = (n_even, 2, S5_GROUPS, S5_STATE)
    s5_a_re = -0.5 + nrm(s5_shape, 0.01)
    s5_a_im = jnp.pi * jnp.arange(S5_STATE, dtype=F32) + nrm(s5_shape, 0.01)
    s5_log_dt = jax.random.uniform(next(ks), s5_shape, F32, math.log(1e-3), math.log(1e-1))
    s5_b_re = nrm(s5_shape + (S5_CH,), (2 * S5_CH) ** -0.5)
    s5_b_im = nrm(s5_shape + (S5_CH,), (2 * S5_CH) ** -0.5)
    s5_c_re = nrm((n_even, 2, S5_GROUPS, S5_CH, S5_STATE), S5_STATE ** -0.5)
    s5_c_im = nrm((n_even, 2, S5_GROUPS, S5_CH, S5_STATE), S5_STATE ** -0.5)
    s5_d = nrm((n_even, S5_GROUPS, S5_CH), 0.5)
    s5_glu_w = nrm((n_even, S5_WIDTH, S5_WIDTH), S5_WIDTH ** -0.5)
    s5_glu_b = nrm((n_even, S5_WIDTH), 0.02)
    ab_w_out = nrm((n_even, AB_OUT, D_MODEL), AB_OUT ** -0.5)
    ssd_w_in = nrm((n_odd, D_MODEL, SSD_IN), D_MODEL ** -0.5)
    ssd_conv_w = nrm((n_odd, SSD_CONV, SSD_XBC), SSD_CONV ** -0.5)
    ssd_conv_b = nrm((n_odd, SSD_XBC), 0.02)
    dt0 = jnp.exp(jax.random.uniform(next(ks), (n_odd, 2, SSD_HEADS), F32, math.log(1e-3), math.log(1e-1)))
    ssd_dt_bias = dt0 + jnp.log(-jnp.expm1(-dt0))
    ssd_a_log = jnp.log(jax.random.uniform(next(ks), (n_odd, 2, SSD_HEADS), F32, 1.0, 16.0))
    ssd_d = 1.0 + nrm((n_odd, SSD_HEADS), 0.02)
    ssd_norm_g = 1.0 + nrm((n_odd, SSD_INNER), 0.02)
    ssd_w_out = nrm((n_odd, SSD_INNER, D_MODEL), SSD_INNER ** -0.5)
    peer_wq = nrm((DEPTH, D_MODEL, PEER_HEADS * PEER_QDIM), D_MODEL ** -0.5)
    peer_keys = nrm((DEPTH, PEER_HEADS, 2, PEER_NKEYS, PEER_HALF), PEER_HALF ** -0.5)
    peer_u = nrm((DEPTH, PEER_EXPERTS, D_MODEL), D_MODEL ** -0.5)
    peer_v = nrm((DEPTH, PEER_EXPERTS, D_MODEL), PEER_HEADS ** -0.5)
    final_norm_g = 1.0 + nrm((D_MODEL,), 0.02)
    return {"x": x, "c": c, "ctx": ctx, "c_ctx": c_ctx, "mod_w": mod_w, "mod_b": mod_b,
            "norm1_g": norm1_g, "norm2_g": norm2_g, "ab_w_in": ab_w_in, "attn_sink": attn_sink,
            "s5_a_re": s5_a_re, "s5_a_im": s5_a_im, "s5_log_dt": s5_log_dt, "s5_b_re": s5_b_re,
            "s5_b_im": s5_b_im, "s5_c_re": s5_c_re, "s5_c_im": s5_c_im, "s5_d": s5_d,
            "s5_glu_w": s5_glu_w, "s5_glu_b": s5_glu_b, "ab_w_out": ab_w_out,
            "ssd_w_in": ssd_w_in, "ssd_conv_w": ssd_conv_w, "ssd_conv_b": ssd_conv_b,
            "ssd_dt_bias": ssd_dt_bias, "ssd_a_log": ssd_a_log, "ssd_d": ssd_d, "ssd_norm_g": ssd_norm_g,
            "ssd_w_out": ssd_w_out, "peer_wq": peer_wq, "peer_keys": peer_keys, "peer_u": peer_u,
            "peer_v": peer_v, "final_norm_g": final_norm_g}


def reference(x, c, ctx, c_ctx, mod_w, mod_b, norm1_g, norm2_g, ab_w_in, attn_sink,
              s5_a_re, s5_a_im, s5_log_dt, s5_b_re, s5_b_im, s5_c_re, s5_c_im, s5_d,
              s5_glu_w, s5_glu_b, ab_w_out, ssd_w_in, ssd_conv_w, ssd_conv_b, ssd_dt_bias,
              ssd_a_log, ssd_d, ssd_norm_g, ssd_w_out, peer_wq, peer_keys, peer_u, peer_v,
              final_norm_g):
    h, hc = x, ctx
    for i in range(DEPTH):
        last = i == DEPTH - 1
        mod_l = jax.nn.silu(c) @ mod_w[i] + mod_b[i]
        mod_c = jax.nn.silu(c_ctx) @ mod_w[i] + mod_b[i]
        sh1, sc1, g1, sh2, sc2, g2 = [m[:, None] for m in jnp.split(mod_l, N_MOD, axis=-1)]
        csh1, csc1, cg1, csh2, csc2, cg2 = jnp.split(mod_c, N_MOD, axis=-1)
        a_lat = modulate(rmsnorm(h, norm1_g[i]), sh1, sc1)
        a_ctx = modulate(rmsnorm(hc, norm1_g[i]), csh1, csc1)
        j = i // 2
        if i % 2 == 0:
            o_lat, o_ctx = attn_s5_mixer(a_lat, a_ctx, ab_w_in[j], attn_sink[j], s5_a_re[j], s5_a_im[j],
                                         s5_log_dt[j], s5_b_re[j], s5_b_im[j], s5_c_re[j], s5_c_im[j],
                                         s5_d[j], s5_glu_w[j], s5_glu_b[j], ab_w_out[j], not last)
        else:
            o_lat, o_ctx = ssd_mixer(a_lat, a_ctx, ssd_w_in[j], ssd_conv_w[j], ssd_conv_b[j], ssd_dt_bias[j],
                                     ssd_a_log[j], ssd_d[j], ssd_norm_g[j], ssd_w_out[j], not last)
        h = h + (g1 * o_lat).astype(h.dtype)
        f_lat = modulate(rmsnorm(h, norm2_g[i]), sh2, sc2)
        h = h + (g2 * peer_ffn(f_lat, peer_wq[i], peer_keys[i], peer_u[i], peer_v[i])).astype(h.dtype)
        if not last:
            hc = hc + (cg1 * o_ctx).astype(hc.dtype)
            f_ctx = modulate(rmsnorm(hc, norm2_g[i]), csh2, csc2)
            hc = hc + (cg2 * peer_ffn(f_ctx, peer_wq[i], peer_keys[i], peer_u[i], peer_v[i])).astype(hc.dtype)
    return rmsnorm(h, final_norm_g)
```

```python
import functools
import math

import jax
import jax.numpy as jnp
from jax import lax
from jax.experimental import pallas as pl
from jax.experimental.pallas import tpu as pltpu

F32 = jnp.float32
BF16 = jnp.bfloat16
EPS = 1e-6

V7X_VMEM_LIMIT_BYTES = 56 * 1024 * 1024
LANES = 128
SUBLANES = 8

N_MOD = 6
HEAD_DIM = 128
ATTN_HEADS = 8
ATTN_KV_HEADS = 2
ATTN_GROUP = ATTN_HEADS // ATTN_KV_HEADS
ATTN_SCALE = HEAD_DIM ** -0.5
BLOCK = 128
GRID_W = 64
ROPE_THETA = 10000.0
S5_CH = 16
S5_STATE = 64
S5_T = 16
SSD_HEAD_DIM = 64
SSD_GROUPS = 8
SSD_STATE = 128
SSD_CONV = 5
SSD_CHUNK = 128
PEER_HEADS = 8
PEER_NKEYS = 128
PEER_HALF = 64
PEER_TOPK = 16
NOT_SELECTED = 127.0


def _cparams(*sem):
    return pltpu.CompilerParams(dimension_semantics=sem, vmem_limit_bytes=V7X_VMEM_LIMIT_BYTES)


def _gelu(x):
    return 0.5 * x * (1.0 + lax.erf(x * (2.0 ** -0.5)))


def _mod_kernel(c_ref, w_ref, b_ref, o_ref):
    s = jax.nn.silu(c_ref[...]).astype(BF16)
    o_ref[...] = jnp.dot(s, w_ref[...].astype(BF16), preferred_element_type=F32) + b_ref[...]


def modulation(cond, w, b, tn=512):
    rows, d = cond.shape
    n = w.shape[1]
    return pl.pallas_call(
        _mod_kernel,
        grid=(n // tn,),
        in_specs=[pl.BlockSpec((rows, d), lambda j: (0, 0)),
                  pl.BlockSpec((d, tn), lambda j: (0, j)),
                  pl.BlockSpec((1, tn), lambda j: (0, j))],
        out_specs=pl.BlockSpec((rows, tn), lambda j: (0, j)),
        out_shape=jax.ShapeDtypeStruct((rows, n), F32),
        compiler_params=_cparams("arbitrary"),
        name="modulation",
    )(cond, w, b.reshape(1, n))


def _nmm_kernel(x_ref, g_ref, sh_ref, sc_ref, w_ref, *refs, emit_a):
    if emit_a:
        o_ref, a_out_ref, a_scr = refs
    else:
        o_ref, a_scr = refs

    @pl.when(pl.program_id(1) == 0)
    def _():
        x = x_ref[...]
        ms = jnp.mean(x * x, axis=-1, keepdims=True)
        y = x * lax.rsqrt(ms + EPS) * g_ref[...]
        a = y * (1.0 + sc_ref[...]) + sh_ref[...]
        a_scr[...] = a.astype(BF16)
        if emit_a:
            a_out_ref[...] = a.astype(BF16)

    o_ref[...] = jnp.dot(a_scr[...], w_ref[...], preferred_element_type=F32).astype(o_ref.dtype)


def norm_mod_matmul(x, g, modr, shift_idx, scale_idx, w, mod_row, tm, tn, out_dtype=F32, emit_a=False):
    m, d = x.shape
    n = w.shape[1]
    out_shape = [jax.ShapeDtypeStruct((m, n), out_dtype)]
    out_specs = [pl.BlockSpec((tm, tn), lambda i, j: (i, j))]
    if emit_a:
        out_shape.append(jax.ShapeDtypeStruct((m, d), BF16))
        out_specs.append(pl.BlockSpec((tm, d), lambda i, j: (i, 0)))
    res = pl.pallas_call(
        functools.partial(_nmm_kernel, emit_a=emit_a),
        grid=(m // tm, n // tn),
        in_specs=[pl.BlockSpec((tm, d), lambda i, j: (i, 0)),
                  pl.BlockSpec((1, d), lambda i, j: (0, 0)),
                  pl.BlockSpec((None, 1, d), lambda i, j: (mod_row(i * tm) * N_MOD + shift_idx, 0, 0)),
                  pl.BlockSpec((None, 1, d), lambda i, j: (mod_row(i * tm) * N_MOD + scale_idx, 0, 0)),
                  pl.BlockSpec((d, tn), lambda i, j: (0, j))],
        out_specs=out_specs,
        out_shape=out_shape,
        scratch_shapes=[pltpu.VMEM((tm, d), BF16)],
        compiler_params=_cparams("arbitrary", "arbitrary"),
        name="norm_mod_matmul",
    )(x, g.reshape(1, d), modr, modr, w)
    return res if emit_a else res[0]


def _mmres_kernel(*refs, n_parts):
    xs = refs[:n_parts]
    ws = refs[n_parts:2 * n_parts]
    gate_ref, res_ref, o_ref = refs[2 * n_parts:]
    acc = jnp.dot(xs[0][...], ws[0][...], preferred_element_type=F32)
    for p in range(1, n_parts):
        acc = acc + jnp.dot(xs[p][...], ws[p][...], preferred_element_type=F32)
    o_ref[...] = res_ref[...] + gate_ref[...] * acc


def matmul_gated_residual(xs, ws, modr, gate_idx, res, mod_row, tm, tn):
    m, n = res.shape
    n_parts = len(xs)
    in_specs = ([pl.BlockSpec((tm, x.shape[1]), lambda i, j: (i, 0)) for x in xs]
                + [pl.BlockSpec((w.shape[0], tn), lambda i, j: (0, j)) for w in ws]
                + [pl.BlockSpec((None, 1, tn), lambda i, j: (mod_row(i * tm) * N_MOD + gate_idx, 0, j)),
                   pl.BlockSpec((tm, tn), lambda i, j: (i, j))])
    return pl.pallas_call(
        functools.partial(_mmres_kernel, n_parts=n_parts),
        grid=(m // tm, n // tn),
        in_specs=in_specs,
        out_specs=pl.BlockSpec((tm, tn), lambda i, j: (i, j)),
        out_shape=jax.ShapeDtypeStruct((m, n), F32),
        compiler_params=_cparams("arbitrary", "arbitrary"),
        name="matmul_gated_residual",
    )(*xs, *ws, modr, res)


def _resid_kernel(o_ref, gate_ref, res_ref, g_ref, out_ref, *, final_norm):
    h = res_ref[...] + gate_ref[...] * o_ref[...]
    if final_norm:
        ms = jnp.mean(h * h, axis=-1, keepdims=True)
        h = h * lax.rsqrt(ms + EPS) * g_ref[...]
    out_ref[...] = h


def gated_residual(o, modr, gate_idx, res, mod_row, tm, norm_g=None):
    m, d = res.shape
    final_norm = norm_g is not None
    g = norm_g if final_norm else jnp.ones((d,), F32)
    return pl.pallas_call(
        functools.partial(_resid_kernel, final_norm=final_norm),
        grid=(m // tm,),
        in_specs=[pl.BlockSpec((tm, d), lambda i: (i, 0)),
                  pl.BlockSpec((None, 1, d), lambda i: (mod_row(i * tm) * N_MOD + gate_idx, 0, 0)),
                  pl.BlockSpec((tm, d), lambda i: (i, 0)),
                  pl.BlockSpec((1, d), lambda i: (0, 0))],
        out_specs=pl.BlockSpec((tm, d), lambda i: (i, 0)),
        out_shape=jax.ShapeDtypeStruct((m, d), F32),
        compiler_params=_cparams("arbitrary"),
        name="gated_residual",
    )(o, modr, res, g.reshape(1, d))


def _rope_kernel(x_ref, cos_ref, sin_ref, o_ref):
    x = x_ref[...]
    o_ref[...] = (x * cos_ref[...] + pltpu.roll(x, HEAD_DIM // 2, 1) * sin_ref[...]).astype(o_ref.dtype)


def rope_tables(seq):
    t = jnp.arange(seq)
    row = (t // GRID_W).astype(F32)
    col = (t % GRID_W).astype(F32)
    n_freq = HEAD_DIM // 4
    inv = ROPE_THETA ** (-jnp.arange(n_freq, dtype=F32) / n_freq)
    ang = jnp.concatenate([row[:, None] * inv, col[:, None] * inv], axis=-1)
    cos, sin = jnp.cos(ang), jnp.sin(ang)
    return jnp.concatenate([cos, cos], axis=-1), jnp.concatenate([-sin, sin], axis=-1)


def rope_qk(p, n_lat_rows, seq, n_cols, tm=512):
    cos, sin = rope_tables(seq)
    tiles_per_seq = seq // tm
    return pl.pallas_call(
        _rope_kernel,
        grid=(n_lat_rows // tm, n_cols // HEAD_DIM),
        in_specs=[pl.BlockSpec((tm, HEAD_DIM), lambda i, j: (i, j)),
                  pl.BlockSpec((tm, HEAD_DIM), lambda i, j: (i % tiles_per_seq, 0)),
                  pl.BlockSpec((tm, HEAD_DIM), lambda i, j: (i % tiles_per_seq, 0))],
        out_specs=pl.BlockSpec((tm, HEAD_DIM), lambda i, j: (i, j)),
        out_shape=jax.ShapeDtypeStruct((n_lat_rows, n_cols), BF16),
        compiler_params=_cparams("arbitrary", "arbitrary"),
        name="rope_qk",
    )(p, cos, sin)


def _attn_group(q, k, v, mask, sink_col):
    s = lax.dot_general(q, k, (((1,), (1,)), ((), ())), preferred_element_type=F32) * ATTN_SCALE
    if mask is not None:
        s = jnp.where(mask, s, -jnp.inf)
    m = jnp.maximum(jnp.max(s, axis=-1, keepdims=True), sink_col)
    e = jnp.exp(s - m)
    den = jnp.sum(e, axis=-1, keepdims=True) + jnp.exp(sink_col - m)
    o = jnp.dot(e.astype(BF16), v, preferred_element_type=F32)
    return o / den


def _sink_col(sink_ref, hk):
    return jnp.concatenate([jnp.full((BLOCK, 1), sink_ref[hk * ATTN_GROUP + g], F32)
                            for g in range(ATTN_GROUP)], axis=0)


def _attn_lat_kernel(sink_ref, q_ref, kp_ref, kc_ref, kn_ref, vp_ref, vc_ref, vn_ref, kx_ref, vx_ref,
                     o_ref, *, n_blocks, n_ctx):
    n = pl.program_id(1)
    row = lax.broadcasted_iota(jnp.int32, (BLOCK, BLOCK), 0)
    col = lax.broadcasted_iota(jnp.int32, (BLOCK, BLOCK), 1)
    m_prev = (col >= row) & (n > 0)
    m_next = (col <= row) & (n < n_blocks - 1)
    ones_ctx = jnp.ones((BLOCK, n_ctx), jnp.bool_)
    ones_cur = jnp.ones((BLOCK, BLOCK), jnp.bool_)
    mask1 = jnp.concatenate([ones_ctx, m_prev, ones_cur, m_next], axis=1)
    mask = jnp.concatenate([mask1] * ATTN_GROUP, axis=0)
    for hk in range(ATTN_KV_HEADS):
        hs = slice(hk * HEAD_DIM, (hk + 1) * HEAD_DIM)
        k = jnp.concatenate([kx_ref[:, hs].astype(BF16), kp_ref[:, hs], kc_ref[:, hs], kn_ref[:, hs]], axis=0)
        v = jnp.concatenate([vx_ref[:, hs], vp_ref[:, hs], vc_ref[:, hs], vn_ref[:, hs]], axis=0).astype(BF16)
        q = jnp.concatenate([q_ref[:, (hk * ATTN_GROUP + g) * HEAD_DIM:(hk * ATTN_GROUP + g + 1) * HEAD_DIM]
                             for g in range(ATTN_GROUP)], axis=0)
        o = _attn_group(q, k, v, mask, _sink_col(sink_ref, hk))
        for g in range(ATTN_GROUP):
            h = hk * ATTN_GROUP + g
            o_ref[:, h * HEAD_DIM:(h + 1) * HEAD_DIM] = o[g * BLOCK:(g + 1) * BLOCK].astype(o_ref.dtype)


def attention_latent(qk, p, sink, bsz, seq, n_ctx):
    nb = seq // BLOCK
    kvw = ATTN_KV_HEADS * HEAD_DIM
    qw = ATTN_HEADS * HEAD_DIM
    kcol, vcol = qw // kvw, (qw + kvw) // kvw
    ctx_blk0 = bsz * seq // n_ctx

    def prev(b, n):
        return b * nb + jnp.maximum(n - 1, 0)

    def nxt(b, n):
        return b * nb + jnp.minimum(n + 1, nb - 1)

    return pl.pallas_call(
        functools.partial(_attn_lat_kernel, n_blocks=nb, n_ctx=n_ctx),
        grid=(bsz, nb),
        in_specs=[pl.BlockSpec(memory_space=pltpu.SMEM),
                  pl.BlockSpec((BLOCK, qw), lambda b, n: (b * nb + n, 0)),
                  pl.BlockSpec((BLOCK, kvw), lambda b, n: (prev(b, n), kcol)),
                  pl.BlockSpec((BLOCK, kvw), lambda b, n: (b * nb + n, kcol)),
                  pl.BlockSpec((BLOCK, kvw), lambda b, n: (nxt(b, n), kcol)),
                  pl.BlockSpec((BLOCK, kvw), lambda b, n: (prev(b, n), vcol)),
                  pl.BlockSpec((BLOCK, kvw), lambda b, n: (b * nb + n, vcol)),
                  pl.BlockSpec((BLOCK, kvw), lambda b, n: (nxt(b, n), vcol)),
                  pl.BlockSpec((n_ctx, kvw), lambda b, n: (ctx_blk0 + b, kcol)),
                  pl.BlockSpec((n_ctx, kvw), lambda b, n: (ctx_blk0 + b, vcol))],
        out_specs=pl.BlockSpec((BLOCK, qw), lambda b, n: (b * nb + n, 0)),
        out_shape=jax.ShapeDtypeStruct((bsz * seq, qw), BF16),
        compiler_params=_cparams("arbitrary", "arbitrary"),
        name="attention_latent",
    )(sink, qk, qk, qk, qk, p, p, p, p, p)


def _attn_ctx_kernel(sink_ref, q_ref, k_ref, v_ref, o_ref, *, n_ctx):
    for hk in range(ATTN_KV_HEADS):
        hs = slice(hk * HEAD_DIM, (hk + 1) * HEAD_DIM)
        k = k_ref[:, hs].astype(BF16)
        v = v_ref[:, hs].astype(BF16)
        for g in range(ATTN_GROUP):
            h = hk * ATTN_GROUP + g
            q = q_ref[:, h * HEAD_DIM:(h + 1) * HEAD_DIM].astype(BF16)
            sink_col = jnp.full((n_ctx, 1), sink_ref[h], F32)
            o = _attn_group(q, k, v, None, sink_col)
            o_ref[:, h * HEAD_DIM:(h + 1) * HEAD_DIM] = o.astype(o_ref.dtype)


def attention_context(p, sink, bsz, seq, n_ctx):
    kvw = ATTN_KV_HEADS * HEAD_DIM
    qw = ATTN_HEADS * HEAD_DIM
    kcol, vcol = qw // kvw, (qw + kvw) // kvw
    ctx_blk0 = bsz * seq // n_ctx
    return pl.pallas_call(
        functools.partial(_attn_ctx_kernel, n_ctx=n_ctx),
        grid=(bsz,),
        in_specs=[pl.BlockSpec(memory_space=pltpu.SMEM),
                  pl.BlockSpec((n_ctx, qw), lambda b: (ctx_blk0 + b, 0)),
                  pl.BlockSpec((n_ctx, kvw), lambda b: (ctx_blk0 + b, kcol)),
                  pl.BlockSpec((n_ctx, kvw), lambda b: (ctx_blk0 + b, vcol))],
        out_specs=pl.BlockSpec((n_ctx, qw), lambda b: (b, 0)),
        out_shape=jax.ShapeDtypeStruct((bsz * n_ctx, qw), BF16),
        compiler_params=_cparams("arbitrary"),
        name="attention_context",
    )(sink, p, p, p)


def _cmul(ar, ai, br, bi):
    return ar * br - ai * bi, ar * bi + ai * br


def s5_group_matrices(a_re, a_im, log_dt, b_re, b_im, c_re, c_im, d):
    hp = lax.Precision.HIGHEST
    t = S5_T
    dt = jnp.exp(log_dt)
    mag = jnp.exp(dt * a_re)
    lr, li = mag * jnp.cos(dt * a_im), mag * jnp.sin(dt * a_im)
    den = a_re * a_re + a_im * a_im
    f_re = ((lr - 1) * a_re + li * a_im) / den
    f_im = (li * a_re - (lr - 1) * a_im) / den
    bb_re = f_re[..., None] * b_re - f_im[..., None] * b_im
    bb_im = f_re[..., None] * b_im + f_im[..., None] * b_re
    pr, pi = [jnp.ones_like(lr)], [jnp.zeros_like(li)]
    for _ in range(t):
        nr, ni = _cmul(pr[-1], pi[-1], lr, li)
        pr.append(nr)
        pi.append(ni)
    pw_re, pw_im = jnp.stack(pr, axis=1), jnp.stack(pi, axis=1)
    cl_re = c_re[:, None] * pw_re[:, :, :, None, :] - c_im[:, None] * pw_im[:, :, :, None, :]
    cl_im = c_re[:, None] * pw_im[:, :, :, None, :] + c_im[:, None] * pw_re[:, :, :, None, :]
    kern = (jnp.einsum('dkgcp,dgpe->dkgce', cl_re, bb_re, precision=hp)
            - jnp.einsum('dkgcp,dgpe->dkgce', cl_im, bb_im, precision=hp))
    g = a_re.shape[1]
    ii = jnp.arange(t)[:, None]
    jj = jnp.arange(t)[None, :]
    kf = kern[0][jnp.clip(jj - ii, 0, t)]
    kb = kern[1][jnp.clip(ii - jj, 0, t)]
    mf = (ii <= jj)[:, :, None, None, None]
    mb = (ii >= jj)[:, :, None, None, None]
    eye = (ii == jj)[:, :, None, None, None] * jnp.eye(S5_CH, dtype=F32)[None, None, None] * d[None, None, :, :, None]
    mm = jnp.where(mf, kf, 0.0) + jnp.where(mb, kb, 0.0) + eye
    m_mat = mm.transpose(2, 0, 4, 1, 3).reshape(g, t * S5_CH, t * S5_CH)
    pf = jnp.arange(t - 1, -1, -1)
    pb = jnp.arange(t)

    def drive(dirn, pidx):
        er, ei = _cmul(pw_re[dirn][pidx][:, :, :, None], pw_im[dirn][pidx][:, :, :, None],
                       bb_re[dirn][None], bb_im[dirn][None])
        er = er.transpose(1, 0, 3, 2).reshape(g, t * S5_CH, S5_STATE)
        ei = ei.transpose(1, 0, 3, 2).reshape(g, t * S5_CH, S5_STATE)
        return er, ei

    ef_re, ef_im = drive(0, pf)
    eb_re, eb_im = drive(1, pb)
    e_mat = jnp.concatenate([ef_re, eb_re, ef_im, eb_im], axis=-1)
    rf = jnp.arange(1, t + 1)
    rb = jnp.arange(t, 0, -1)

    def read(dirn, pidx):
        rr = cl_re[dirn][pidx]
        ri = cl_im[dirn][pidx]
        rr = rr.transpose(1, 3, 0, 2).reshape(g, S5_STATE, t * S5_CH)
        ri = ri.transpose(1, 3, 0, 2).reshape(g, S5_STATE, t * S5_CH)
        return rr, -ri

    rf_re, rf_im = read(0, rf)
    rb_re, rb_im = read(1, rb)
    r_mat = jnp.concatenate([rf_re, rb_re, rf_im, rb_im], axis=1)
    lam = jnp.stack([jnp.concatenate([pw_re[0, t], pw_re[1, t]], axis=-1),
                     jnp.concatenate([pw_im[0, t], pw_im[1, t]], axis=-1)], axis=1)
    return m_mat.astype(BF16), e_mat.astype(BF16), r_mat.astype(BF16), lam


def _s5_kernel(u_ref, m_ref, e_ref, r_ref, lam_ref, y_ref, d_scr, sp_scr, *, n_ctx_chunks, n_chunks):
    u = u_ref[...]
    d_scr[...] = jnp.dot(u, e_ref[...], preferred_element_type=F32)
    two_p = 2 * S5_STATE
    lam_re = jnp.broadcast_to(lam_ref[0:1, :], (SUBLANES, two_p))
    lam_im = jnp.broadcast_to(lam_ref[1:2, :], (SUBLANES, two_p))
    fwd_lane = lax.broadcasted_iota(jnp.int32, (SUBLANES, two_p), 1) < S5_STATE

    def rows(n):
        return pl.ds(pl.multiple_of(n * SUBLANES, SUBLANES), SUBLANES)

    def step(s, carry):
        s_re, s_im = carry
        nf = s
        nb = jnp.where(s < n_ctx_chunks, n_ctx_chunks - 1 - s, n_chunks - 1 - (s - n_ctx_chunks))
        sp_scr[rows(nf), 0:S5_STATE] = s_re[:, 0:S5_STATE]
        sp_scr[rows(nf), two_p:two_p + S5_STATE] = s_im[:, 0:S5_STATE]
        sp_scr[rows(nb), S5_STATE:two_p] = s_re[:, S5_STATE:two_p]
        sp_scr[rows(nb), two_p + S5_STATE:2 * two_p] = s_im[:, S5_STATE:two_p]
        d_re = jnp.where(fwd_lane, d_scr[rows(nf), 0:two_p], d_scr[rows(nb), 0:two_p])
        d_im = jnp.where(fwd_lane, d_scr[rows(nf), two_p:2 * two_p], d_scr[rows(nb), two_p:2 * two_p])
        n_re = lam_re * s_re - lam_im * s_im + d_re
        n_im = lam_re * s_im + lam_im * s_re + d_im
        return n_re, n_im

    zero = jnp.zeros((SUBLANES, two_p), F32)
    lax.fori_loop(0, n_chunks, step, (zero, zero))
    y_ref[...] = (jnp.dot(u, m_ref[...], preferred_element_type=F32)
                  + jnp.dot(sp_scr[...].astype(BF16), r_ref[...], preferred_element_type=F32))


def s5_scan(u_g, m_mat, e_mat, r_mat, lam, n_ctx_chunks, n_chunks):
    g, r, w = u_g.shape
    return pl.pallas_call(
        functools.partial(_s5_kernel, n_ctx_chunks=n_ctx_chunks, n_chunks=n_chunks),
        grid=(g,),
        in_specs=[pl.BlockSpec((None, r, w), lambda i: (i, 0, 0)),
                  pl.BlockSpec((None, w, w), lambda i: (i, 0, 0)),
                  pl.BlockSpec((None, w, 4 * S5_STATE), lambda i: (i, 0, 0)),
                  pl.BlockSpec((None, 4 * S5_STATE, w), lambda i: (i, 0, 0)),
                  pl.BlockSpec((None, 2, 2 * S5_STATE), lambda i: (i, 0, 0))],
        out_specs=pl.BlockSpec((None, r, w), lambda i: (i, 0, 0)),
        out_shape=jax.ShapeDtypeStruct((g, r, w), F32),
        scratch_shapes=[pltpu.VMEM((r, 4 * S5_STATE), F32), pltpu.VMEM((r, 4 * S5_STATE), F32)],
        compiler_params=_cparams("arbitrary"),
        name="s5_scan",
    )(u_g, m_mat, e_mat, r_mat, lam)


def s5_layer(u_all, bsz, seq, n_ctx, params):
    m_mat, e_mat, r_mat, lam = s5_group_matrices(*params)
    width = u_all.shape[1]
    g = width // S5_CH
    t = S5_T

    def to_groups(u, length):
        u = u.reshape(bsz, length // t, t, g, S5_CH).transpose(3, 1, 0, 2, 4)
        u = jnp.pad(u, ((0, 0), (0, 0), (0, SUBLANES - bsz), (0, 0), (0, 0)))
        return u.reshape(g, length // t, SUBLANES, t * S5_CH)

    n_lat = bsz * seq
    ncc, ncl = n_ctx // t, seq // t
    u_g = jnp.concatenate([to_groups(u_all[n_lat:], n_ctx), to_groups(u_all[:n_lat], seq)], axis=1)
    u_g = u_g.reshape(g, (ncc + ncl) * SUBLANES, t * S5_CH).astype(BF16)
    y_g = s5_scan(u_g, m_mat, e_mat, r_mat, lam, ncc, ncc + ncl)
    y_g = y_g.reshape(g, ncc + ncl, SUBLANES, t, S5_CH)[:, :, :bsz]

    def from_groups(y, length):
        return y.transpose(2, 1, 3, 0, 4).reshape(bsz * length, width)

    return jnp.concatenate([from_groups(y_g[:, ncc:], seq), from_groups(y_g[:, :ncc], n_ctx)], axis=0)


def _glu_kernel(y_ref, w_ref, b_ref, o_ref):
    gl = _gelu(y_ref[...])
    z = jnp.dot(gl.astype(BF16), w_ref[...], preferred_element_type=F32) + b_ref[...]
    o_ref[...] = (gl * jax.nn.sigmoid(z)).astype(o_ref.dtype)


def s5_glu(y, w, b, tm=512):
    m, n = y.shape
    return pl.pallas_call(
        _glu_kernel,
        grid=(m // tm,),
        in_specs=[pl.BlockSpec((tm, n), lambda i: (i, 0)),
                  pl.BlockSpec((n, n), lambda i: (0, 0)),
                  pl.BlockSpec((1, n), lambda i: (0, 0))],
        out_specs=pl.BlockSpec((tm, n), lambda i: (i, 0)),
        out_shape=jax.ShapeDtypeStruct((m, n), BF16),
        compiler_params=_cparams("arbitrary"),
        name="s5_glu",
    )(y, w, b.reshape(1, n))


def _top_rows(s, n_take):
    r = s.shape[0]
    rows = lax.broadcasted_iota(jnp.int32, s.shape, 0).astype(F32)
    rank = jnp.full(s.shape, NOT_SELECTED, F32)
    vals, picks = [], []
    for k in range(n_take):
        m = jnp.max(s, axis=0, keepdims=True)
        idx = jnp.min(jnp.where(s == m, rows, float(r)), axis=0, keepdims=True)
        sel = rows == idx
        rank = jnp.where(sel, float(k), rank)
        s = jnp.where(sel, -jnp.inf, s)
        vals.append(m)
        picks.append(idx)
    return vals, picks, rank


def _peer_route_kernel(q_ref, k1_ref, k2_ref, thr_ref, a_ref, r2_ref, b_ref):
    hp = lax.Precision.HIGHEST
    for h in range(PEER_HEADS):
        qh = q_ref[:, h * 2 * PEER_HALF:(h + 1) * 2 * PEER_HALF]
        nt = (((1,), (1,)), ((), ()))
        s1 = lax.dot_general(k1_ref[h], qh, nt, precision=hp, preferred_element_type=F32)
        s2 = lax.dot_general(k2_ref[h], qh, nt, precision=hp, preferred_element_type=F32)
        v1, _, rank1 = _top_rows(s1, PEER_TOPK)
        v2, _, rank2 = _top_rows(s2, PEER_TOPK)
        v2m = jnp.concatenate(v2, axis=0)
        cand = jnp.concatenate([v1[i] + v2m for i in range(PEER_TOPK)], axis=0)
        cv, picks, _ = _top_rows(cand, PEER_TOPK)
        i16 = lax.broadcasted_iota(jnp.int32, (PEER_TOPK, cand.shape[1]), 0).astype(F32)
        cnt = jnp.zeros((PEER_TOPK, cand.shape[1]), F32)
        z = jnp.zeros_like(cv[0])
        for k in range(PEER_TOPK):
            cnt = cnt + jnp.where(i16 == jnp.floor(picks[k] * (1.0 / PEER_TOPK)), 1.0, 0.0)
            z = z + jnp.exp(cv[k] - cv[0])
        thr = jnp.zeros_like(s1)
        for i in range(PEER_TOPK):
            thr = jnp.where(rank1 == float(i), cnt[i:i + 1, :], thr)
        thr_ref[h] = thr
        a_ref[h] = jnp.exp(s1 - v1[0])
        r2_ref[h] = rank2
        b_ref[h] = jnp.exp(s2 - v2[0]) / z


def peer_route(q, keys, tt=256):
    t = q.shape[0]
    zeros = jnp.zeros((PEER_HEADS, PEER_NKEYS, PEER_HALF), F32)
    k1 = jnp.concatenate([keys[:, 0], zeros], axis=-1)
    k2 = jnp.concatenate([zeros, keys[:, 1]], axis=-1)
    shp = jax.ShapeDtypeStruct((PEER_HEADS, PEER_NKEYS, t), F32)
    ospec = pl.BlockSpec((PEER_HEADS, PEER_NKEYS, tt), lambda i: (0, 0, i))
    kspec = pl.BlockSpec((PEER_HEADS, PEER_NKEYS, 2 * PEER_HALF), lambda i: (0, 0, 0))
    return pl.pallas_call(
        _peer_route_kernel,
        grid=(t // tt,),
        in_specs=[pl.BlockSpec((tt, PEER_HEADS * 2 * PEER_HALF), lambda i: (i, 0)), kspec, kspec],
        out_specs=[ospec] * 4,
        out_shape=[shp] * 4,
        compiler_params=_cparams("arbitrary"),
        name="peer_route",
    )(q, k1, k2)


def _peer_dense_kernel(ft_ref, u_ref, vt_ref, thr_ref, a_ref, r2_ref, b_ref, o_ref, *, rows_per_tile):
    act = jnp.dot(u_ref[...], ft_ref[...], preferred_element_type=F32)
    ge = _gelu(act)
    w_rows = []
    for r in range(rows_per_tile):
        w = None
        for h in range(PEER_HEADS):
            term = jnp.where(r2_ref[h] < thr_ref[h, r:r + 1, :], b_ref[h], 0.0) * a_ref[h, r:r + 1, :]
            w = term if w is None else w + term
        w_rows.append((w * ge[r * PEER_NKEYS:(r + 1) * PEER_NKEYS]).astype(BF16))
    wt = jnp.concatenate(w_rows, axis=0)
    contrib = jnp.dot(vt_ref[...], wt, preferred_element_type=F32)

    @pl.when(pl.program_id(1) == 0)
    def _():
        o_ref[...] = contrib

    @pl.when(pl.program_id(1) != 0)
    def _():
        o_ref[...] += contrib


def peer_dense(ft, u, vt, thr, a, r2, b, tt=512, rows_per_tile=8):
    d, t = ft.shape
    e = u.shape[0]
    te = rows_per_tile * PEER_NKEYS
    rspec = pl.BlockSpec((PEER_HEADS, rows_per_tile, tt), lambda i, j: (0, j, i))
    cspec = pl.BlockSpec((PEER_HEADS, PEER_NKEYS, tt), lambda i, j: (0, 0, i))
    return pl.pallas_call(
        functools.partial(_peer_dense_kernel, rows_per_tile=rows_per_tile),
        grid=(t // tt, e // te),
        in_specs=[pl.BlockSpec((d, tt), lambda i, j: (0, i)),
                  pl.BlockSpec((te, d), lambda i, j: (j, 0)),
                  pl.BlockSpec((d, te), lambda i, j: (0, j)),
                  rspec, rspec, cspec, cspec],
        out_specs=pl.BlockSpec((d, tt), lambda i, j: (0, i)),
        out_shape=jax.ShapeDtypeStruct((d, t), F32),
        compiler_params=_cparams("arbitrary", "arbitrary"),
        name="peer_dense",
    )(ft, u, vt, thr, a, r2, b)


def peer_ffn(h, modr, mod_row, norm_g, wq, keys, u, v, tm, final_norm_g=None):
    q, f = norm_mod_matmul(h, norm_g, modr, 3, 4, wq.astype(BF16), mod_row, tm=tm, tn=512, emit_a=True)
    thr, a, r2, b = peer_route(q, keys)
    out_t = peer_dense(f.T, u.astype(BF16), v.T.astype(BF16), thr, a, r2, b)
    return gated_residual(out_t.T, modr, 5, h, mod_row, tm=tm, norm_g=final_norm_g)


def _conv_kernel(x_ref, w_ref, b_ref, o_ref, pad_scr, *, length, row_tile):
    half = SSD_CONV // 2
    zeros = jnp.zeros((SUBLANES, x_ref.shape[1]), F32)
    pad_scr[0:SUBLANES, :] = zeros
    pad_scr[SUBLANES + length:2 * SUBLANES + length, :] = zeros
    pad_scr[SUBLANES:SUBLANES + length, :] = x_ref[...].astype(F32)
    for r0 in range(0, length, row_tile):
        acc = jnp.broadcast_to(b_ref[...], (row_tile, x_ref.shape[1]))
        for k in range(SSD_CONV):
            start = SUBLANES + r0 + k - half
            acc = acc + w_ref[k:k + 1, :] * pad_scr[start:start + row_tile, :]
        o_ref[r0:r0 + row_tile, :] = jax.nn.silu(acc).astype(o_ref.dtype)


def ssd_conv(p, col0, width, row0, n_seq, length, conv_w, conv_b, tc=256):
    row_tile = min(length, 512)
    return pl.pallas_call(
        functools.partial(_conv_kernel, length=length, row_tile=row_tile),
        grid=(n_seq, width // tc),
        in_specs=[pl.BlockSpec((length, tc), lambda s, j: (row0 // length + s, col0 // tc + j)),
                  pl.BlockSpec((SSD_CONV, tc), lambda s, j: (0, j)),
                  pl.BlockSpec((1, tc), lambda s, j: (0, j))],
        out_specs=pl.BlockSpec((length, tc), lambda s, j: (s, j)),
        out_shape=jax.ShapeDtypeStruct((n_seq * length, width), BF16),
        scratch_shapes=[pltpu.VMEM((length + 2 * SUBLANES, tc), F32)],
        compiler_params=_cparams("arbitrary", "arbitrary"),
        name="ssd_conv",
    )(p, conv_w, conv_b.reshape(1, width))


def _ssd_kernel(x_ref, b_ref, c_ref, dt_ref, bias_ref, alog_ref, y_ref, st_scr, *, reverse):
    hp = lax.Precision.HIGHEST
    ch = SSD_CHUNK
    nh = dt_ref.shape[1]
    hpg = nh // SSD_GROUPS

    @pl.when(pl.program_id(1) == 0)
    def _():
        st_scr[...] = jnp.zeros_like(st_scr)

    dt = jax.nn.softplus(dt_ref[...] + bias_ref[...])
    la = dt * (-jnp.exp(alog_ref[...]))
    row = lax.broadcasted_iota(jnp.int32, (ch, ch), 0)
    col = lax.broadcasted_iota(jnp.int32, (ch, ch), 1)
    keep = (col >= row) if reverse else (row >= col)
    tri = jnp.where(keep, 1.0, 0.0)
    cs = jnp.dot(tri, la, precision=hp, preferred_element_type=F32)
    both_t = jnp.concatenate([cs, dt], axis=1).T
    cs_t, dt_t = both_t[0:nh], both_t[nh:2 * nh]
    end = 0 if reverse else ch - 1
    tot_t = cs_t[:, end:end + 1]
    ws_t = jnp.exp(tot_t - cs_t) * dt_t
    dec_t = jnp.exp(tot_t)
    ecs = jnp.exp(cs)
    lo = lax.broadcasted_iota(jnp.int32, (ch, 2 * SSD_HEAD_DIM), 1) < SSD_HEAD_DIM
    nt = (((1,), (1,)), ((), ()))
    for g in range(SSD_GROUPS):
        gs = slice(g * SSD_STATE, (g + 1) * SSD_STATE)
        cg = c_ref[:, gs]
        bg = b_ref[:, gs]
        cb = lax.dot_general(cg, bg, nt, preferred_element_type=F32)
        bg_t = bg.astype(F32).T
        cg32 = cg.astype(F32)
        for pr in range(hpg // 2):
            lanes = slice((g * hpg + 2 * pr) * SSD_HEAD_DIM, (g * hpg + 2 * pr + 2) * SSD_HEAD_DIM)
            sl = slice(2 * pr * SSD_HEAD_DIM, (2 * pr + 2) * SSD_HEAD_DIM)
            xp = x_ref[:, lanes]
            st = st_scr[g, :, sl]
            rhs = jnp.concatenate([xp, st.astype(BF16)], axis=0)
            ys, news, decs = [], [], []
            for j in range(2):
                h = g * hpg + 2 * pr + j
                seg = cs[:, h:h + 1] - cs_t[h:h + 1, :]
                gmat = jnp.where(keep, jnp.exp(seg), 0.0) * cb * dt_t[h:h + 1, :]
                cmat = cg32 * ecs[:, h:h + 1]
                lhs = jnp.concatenate([gmat, cmat], axis=1).astype(BF16)
                ys.append(jnp.dot(lhs, rhs, preferred_element_type=F32))
                bw = (bg_t * ws_t[h:h + 1, :]).astype(BF16)
                news.append(jnp.dot(bw, xp, preferred_element_type=F32))
                decs.append(jnp.broadcast_to(dec_t[h:h + 1, :], (ch, 2 * SSD_HEAD_DIM)))
            y_ref[:, lanes] = jnp.where(lo, ys[0], ys[1]).astype(y_ref.dtype)
            st_scr[g, :, sl] = st * jnp.where(lo, decs[0], decs[1]) + jnp.where(lo, news[0], news[1])


def ssd_scan(xc, dt_raw, dt_bias, a_log, bsz, seq, n_ctx, reverse):
    nh = dt_raw.shape[1]
    inner = nh * SSD_HEAD_DIM
    gn = SSD_GROUPS * SSD_STATE
    nl, nc = seq // SSD_CHUNK, n_ctx // SSD_CHUNK
    lat_blocks = bsz * nl

    def blk(b, s):
        if reverse:
            return jnp.where(s < nc, lat_blocks + b * nc + (nc - 1 - s), b * nl + (nl - 1 - (s - nc)))
        return jnp.where(s < nc, lat_blocks + b * nc + s, b * nl + (s - nc))

    def oblk(b, s):
        sl = jnp.maximum(s - nc, 0)
        return b * nl + ((nl - 1 - sl) if reverse else sl)

    return pl.pallas_call(
        functools.partial(_ssd_kernel, reverse=reverse),
        grid=(bsz, nc + nl),
        in_specs=[pl.BlockSpec((SSD_CHUNK, inner), lambda b, s: (blk(b, s), 0)),
                  pl.BlockSpec((SSD_CHUNK, gn), lambda b, s: (blk(b, s), inner // gn)),
                  pl.BlockSpec((SSD_CHUNK, gn), lambda b, s: (blk(b, s), inner // gn + 1)),
                  pl.BlockSpec((SSD_CHUNK, nh), lambda b, s: (blk(b, s), 0)),
                  pl.BlockSpec((1, nh), lambda b, s: (0, 0)),
                  pl.BlockSpec((1, nh), lambda b, s: (0, 0))],
        out_specs=pl.BlockSpec((SSD_CHUNK, inner), lambda b, s: (oblk(b, s), 0)),
        out_shape=jax.ShapeDtypeStruct((bsz * seq, inner), BF16),
        scratch_shapes=[pltpu.VMEM((SSD_GROUPS, SSD_STATE, inner // SSD_GROUPS), F32)],
        compiler_params=_cparams("arbitrary", "arbitrary"),
        name="ssd_scan_bwd" if reverse else "ssd_scan_fwd",
    )(xc, xc, xc, dt_raw, dt_bias.reshape(1, nh), a_log.reshape(1, nh))


def _ssd_out_kernel(yf_ref, yb_ref, x_ref, z_ref, d_ref, ng_ref, w_ref, gate_ref, res_ref, o_ref, a_scr):
    @pl.when(pl.program_id(1) == 0)
    def _():
        y = (x_ref[...].astype(F32) * d_ref[...] + yf_ref[...].astype(F32) + yb_ref[...].astype(F32))
        gv = y * jax.nn.silu(z_ref[...].astype(F32))
        gw = gv.shape[1] // SSD_GROUPS
        for g in range(SSD_GROUPS):
            part = gv[:, g * gw:(g + 1) * gw]
            ms = jnp.mean(part * part, axis=-1, keepdims=True)
            a_scr[:, g * gw:(g + 1) * gw] = (part * lax.rsqrt(ms + EPS) * ng_ref[:, g * gw:(g + 1) * gw]).astype(BF16)

    acc = jnp.dot(a_scr[...], w_ref[...], preferred_element_type=F32)
    o_ref[...] = res_ref[...] + gate_ref[...] * acc


def ssd_output(yf, yb, xc, pz, d_exp, norm_g, w_out, modr, res, mod_row, tm=256, tn=512):
    m, inner = yf.shape
    n = w_out.shape[1]
    return pl.pallas_call(
        _ssd_out_kernel,
        grid=(m // tm, n // tn),
        in_specs=[pl.BlockSpec((tm, inner), lambda i, j: (i, 0)),
                  pl.BlockSpec((tm, inner), lambda i, j: (i, 0)),
                  pl.BlockSpec((tm, inner), lambda i, j: (i, 0)),
                  pl.BlockSpec((tm, inner), lambda i, j: (i, 0)),
                  pl.BlockSpec((1, inner), lambda i, j: (0, 0)),
                  pl.BlockSpec((1, inner), lambda i, j: (0, 0)),
                  pl.BlockSpec((inner, tn), lambda i, j: (0, j)),
                  pl.BlockSpec((None, 1, tn), lambda i, j: (mod_row(i * tm) * N_MOD + 2, 0, j)),
                  pl.BlockSpec((tm, tn), lambda i, j: (i, j))],
        out_specs=pl.BlockSpec((tm, tn), lambda i, j: (i, j)),
        out_shape=jax.ShapeDtypeStruct((m, n), F32),
        scratch_shapes=[pltpu.VMEM((tm, inner), BF16)],
        compiler_params=_cparams("arbitrary", "arbitrary"),
        name="ssd_output",
    )(yf, yb, xc, pz, d_exp.reshape(1, inner), norm_g.reshape(1, inner), w_out, modr, res)


def _mod_rows(c, c_ctx, w, b):
    bsz, d = c.shape
    cond = jnp.zeros((SUBLANES, d), F32).at[:bsz].set(c).at[bsz].set(c_ctx)
    mod = modulation(cond, w, b)
    return mod.reshape(SUBLANES * N_MOD, 1, d)


def attn_s5_layer(h, bsz, seq, n_ctx, modr, mod_row, tm, norm1_g, w_in, sink, s5_params, glu_w, glu_b, w_out):
    n_lat = bsz * seq
    qw = ATTN_HEADS * HEAD_DIM
    kvw = ATTN_KV_HEADS * HEAD_DIM
    p = norm_mod_matmul(h, norm1_g, modr, 0, 1, w_in.astype(BF16), mod_row, tm=tm, tn=512)
    qk = rope_qk(p, n_lat, seq, qw + kvw)
    attn_lat = attention_latent(qk, p, sink, bsz, seq, n_ctx)
    attn_ctx = attention_context(p, sink, bsz, seq, n_ctx)
    attn = jnp.concatenate([attn_lat, attn_ctx], axis=0)
    y = s5_layer(p[:, qw + 2 * kvw:], bsz, seq, n_ctx, s5_params)
    s5o = s5_glu(y, glu_w.astype(BF16), glu_b)
    w_out = w_out.astype(BF16)
    return matmul_gated_residual([attn, s5o], [w_out[:qw], w_out[qw:]], modr, 2, h, mod_row, tm=tm, tn=512)


def ssd_layer(h, bsz, seq, n_ctx, modr, mod_row, tm, norm1_g, w_in, conv_w, conv_b, dt_bias, a_log, d_skip,
              norm_g, w_out):
    n_lat = bsz * seq
    nh = a_log.shape[1]
    inner = nh * SSD_HEAD_DIM
    xbc_w = inner + 2 * SSD_GROUPS * SSD_STATE
    w_in = w_in.astype(BF16)
    pz = norm_mod_matmul(h, norm1_g, modr, 0, 1, w_in[:, :inner + xbc_w], mod_row, tm=tm, tn=1024)
    dt_raw = norm_mod_matmul(h, norm1_g, modr, 0, 1, w_in[:, inner + xbc_w:], mod_row, tm=tm, tn=2 * nh)
    xc = jnp.concatenate([ssd_conv(pz, inner, xbc_w, 0, bsz, seq, conv_w, conv_b),
                          ssd_conv(pz, inner, xbc_w, n_lat, bsz, n_ctx, conv_w, conv_b)], axis=0)
    yf = ssd_scan(xc, dt_raw[:, :nh], dt_bias[0], a_log[0], bsz, seq, n_ctx, reverse=False)
    yb = ssd_scan(xc, dt_raw[:, nh:], dt_bias[1], a_log[1], bsz, seq, n_ctx, reverse=True)
    d_exp = jnp.repeat(d_skip, SSD_HEAD_DIM)
    return ssd_output(yf, yb, xc, pz, d_exp, norm_g, w_out.astype(BF16), modr, h[:n_lat], mod_row)


def kernel(x, c, ctx, c_ctx, mod_w, mod_b, norm1_g, norm2_g, ab_w_in, attn_sink, s5_a_re, s5_a_im, s5_log_dt, s5_b_re, s5_b_im, s5_c_re, s5_c_im, s5_d, s5_glu_w, s5_glu_b, ab_w_out, ssd_w_in, ssd_conv_w, ssd_conv_b, ssd_dt_bias, ssd_a_log, ssd_d, ssd_norm_g, ssd_w_out, peer_wq, peer_keys, peer_u, peer_v, final_norm_g):
    bsz, seq, d = x.shape
    n_ctx = ctx.shape[1]
    n_lat = bsz * seq
    depth = mod_w.shape[0]
    tm = 512

    def mod_row(r):
        return jnp.minimum(r // seq, bsz)

    h = jnp.concatenate([x.reshape(n_lat, d), ctx.reshape(bsz * n_ctx, d)], axis=0)
    for i in range(depth):
        last = i == depth - 1
        j = i // 2
        modr = _mod_rows(c, c_ctx, mod_w[i], mod_b[i])
        if i % 2 == 0:
            s5_params = (s5_a_re[j], s5_a_im[j], s5_log_dt[j], s5_b_re[j], s5_b_im[j], s5_c_re[j], s5_c_im[j],
                         s5_d[j])
            h = attn_s5_layer(h, bsz, seq, n_ctx, modr, mod_row, tm, norm1_g[i], ab_w_in[j], attn_sink[j],
                              s5_params, s5_glu_w[j], s5_glu_b[j], ab_w_out[j])
        else:
            if not last:
                raise NotImplementedError("an SSD layer that must also update the context stream")
            h = ssd_layer(h, bsz, seq, n_ctx, modr, mod_row, tm, norm1_g[i], ssd_w_in[j], ssd_conv_w[j],
                          ssd_conv_b[j], ssd_dt_bias[j], ssd_a_log[j], ssd_d[j], ssd_norm_g[j], ssd_w_out[j])
        h = peer_ffn(h, modr, mod_row, norm2_g[i], peer_wq[i], peer_keys[i], peer_u[i], peer_v[i], tm,
                     final_norm_g=final_norm_g if last else None)
    return h[:n_lat].reshape(bsz, seq, d)
```

```python
import functools
import math

import jax
import jax.numpy as jnp
from jax import lax
from jax.experimental import pallas as pl
from jax.experimental.pallas import tpu as pltpu

F32 = jnp.float32
BF16 = jnp.bfloat16
EPS = 1e-6

V7X_VMEM_LIMIT_BYTES = 56 * 1024 * 1024
LANES = 128
SUBLANES = 8

N_MOD = 6
HEAD_DIM = 128
ATTN_HEADS = 8
ATTN_KV_HEADS = 2
ATTN_GROUP = ATTN_HEADS // ATTN_KV_HEADS
ATTN_SCALE = HEAD_DIM ** -0.5
BLOCK = 128
GRID_W = 64
ROPE_THETA = 10000.0
S5_CH = 16
S5_STATE = 64
S5_T = 16
S5_GB = 2
SSD_HEAD_DIM = 64
SSD_GROUPS = 8
SSD_STATE = 128
SSD_CONV = 5
SSD_CHUNK = 128
PEER_HEADS = 8
PEER_NKEYS = 128
PEER_HALF = 64
PEER_TOPK = 16
NOT_SELECTED = 127.0


def _cparams(*sem):
    return pltpu.CompilerParams(dimension_semantics=sem, vmem_limit_bytes=V7X_VMEM_LIMIT_BYTES)


def _gelu(x):
    return 0.5 * x * (1.0 + lax.erf(x * (2.0 ** -0.5)))


def _mod_kernel(c_ref, w_ref, b_ref, o_ref):
    s = jax.nn.silu(c_ref[...]).astype(BF16)
    o_ref[...] = jnp.dot(s, w_ref[...].astype(BF16), preferred_element_type=F32) + b_ref[...]


def modulation(cond, w, b, tn=512):
    rows, d = cond.shape
    n = w.shape[1]
    return pl.pallas_call(
        _mod_kernel,
        grid=(n // tn,),
        in_specs=[pl.BlockSpec((rows, d), lambda j: (0, 0)),
                  pl.BlockSpec((d, tn), lambda j: (0, j)),
                  pl.BlockSpec((1, tn), lambda j: (0, j))],
        out_specs=pl.BlockSpec((rows, tn), lambda j: (0, j)),
        out_shape=jax.ShapeDtypeStruct((rows, n), F32),
        compiler_params=_cparams("arbitrary"),
        name="modulation",
    )(cond, w, b.reshape(1, n))


def _nmm_kernel(x_ref, g_ref, sh_ref, sc_ref, w_ref, *refs, emit_a):
    if emit_a:
        o_ref, a_out_ref, a_scr = refs
    else:
        o_ref, a_scr = refs

    @pl.when(pl.program_id(1) == 0)
    def _():
        x = x_ref[...]
        ms = jnp.mean(x * x, axis=-1, keepdims=True)
        y = x * lax.rsqrt(ms + EPS) * g_ref[...]
        a = y * (1.0 + sc_ref[...]) + sh_ref[...]
        a_scr[...] = a.astype(BF16)
        if emit_a:
            a_out_ref[...] = a.astype(BF16)

    o_ref[...] = jnp.dot(a_scr[...], w_ref[...], preferred_element_type=F32).astype(o_ref.dtype)


def norm_mod_matmul(x, g, modr, shift_idx, scale_idx, w, mod_row, tm, tn, out_dtype=F32, emit_a=False):
    m, d = x.shape
    n = w.shape[1]
    out_shape = [jax.ShapeDtypeStruct((m, n), out_dtype)]
    out_specs = [pl.BlockSpec((tm, tn), lambda i, j: (i, j))]
    if emit_a:
        out_shape.append(jax.ShapeDtypeStruct((m, d), BF16))
        out_specs.append(pl.BlockSpec((tm, d), lambda i, j: (i, 0)))
    res = pl.pallas_call(
        functools.partial(_nmm_kernel, emit_a=emit_a),
        grid=(m // tm, n // tn),
        in_specs=[pl.BlockSpec((tm, d), lambda i, j: (i, 0)),
                  pl.BlockSpec((1, d), lambda i, j: (0, 0)),
                  pl.BlockSpec((None, 1, d), lambda i, j: (mod_row(i * tm) * N_MOD + shift_idx, 0, 0)),
                  pl.BlockSpec((None, 1, d), lambda i, j: (mod_row(i * tm) * N_MOD + scale_idx, 0, 0)),
                  pl.BlockSpec((d, tn), lambda i, j: (0, j))],
        out_specs=out_specs,
        out_shape=out_shape,
        scratch_shapes=[pltpu.VMEM((tm, d), BF16)],
        compiler_params=_cparams("arbitrary", "arbitrary"),
        name="norm_mod_matmul",
    )(x, g.reshape(1, d), modr, modr, w)
    return res if emit_a else res[0]


def _mmres_kernel(*refs, n_parts):
    xs = refs[:n_parts]
    ws = refs[n_parts:2 * n_parts]
    gate_ref, res_ref, o_ref = refs[2 * n_parts:]
    acc = jnp.dot(xs[0][...], ws[0][...], preferred_element_type=F32)
    for p in range(1, n_parts):
        acc = acc + jnp.dot(xs[p][...], ws[p][...], preferred_element_type=F32)
    o_ref[...] = res_ref[...] + gate_ref[...] * acc


def matmul_gated_residual(xs, ws, modr, gate_idx, res, mod_row, tm, tn):
    m, n = res.shape
    n_parts = len(xs)
    in_specs = ([pl.BlockSpec((tm, x.shape[1]), lambda i, j: (i, 0)) for x in xs]
                + [pl.BlockSpec((w.shape[0], tn), lambda i, j: (0, j)) for w in ws]
                + [pl.BlockSpec((None, 1, tn), lambda i, j: (mod_row(i * tm) * N_MOD + gate_idx, 0, j)),
                   pl.BlockSpec((tm, tn), lambda i, j: (i, j))])
    return pl.pallas_call(
        functools.partial(_mmres_kernel, n_parts=n_parts),
        grid=(m // tm, n // tn),
        in_specs=in_specs,
        out_specs=pl.BlockSpec((tm, tn), lambda i, j: (i, j)),
        out_shape=jax.ShapeDtypeStruct((m, n), F32),
        compiler_params=_cparams("arbitrary", "arbitrary"),
        name="matmul_gated_residual",
    )(*xs, *ws, modr, res)


def _resid_kernel(o_ref, gate_ref, res_ref, g_ref, out_ref, *, final_norm):
    h = res_ref[...] + gate_ref[...] * o_ref[...]
    if final_norm:
        ms = jnp.mean(h * h, axis=-1, keepdims=True)
        h = h * lax.rsqrt(ms + EPS) * g_ref[...]
    out_ref[...] = h


def gated_residual(o, modr, gate_idx, res, mod_row, tm, norm_g=None):
    m, d = res.shape
    final_norm = norm_g is not None
    g = norm_g if final_norm else jnp.ones((d,), F32)
    return pl.pallas_call(
        functools.partial(_resid_kernel, final_norm=final_norm),
        grid=(m // tm,),
        in_specs=[pl.BlockSpec((tm, d), lambda i: (i, 0)),
                  pl.BlockSpec((None, 1, d), lambda i: (mod_row(i * tm) * N_MOD + gate_idx, 0, 0)),
                  pl.BlockSpec((tm, d), lambda i: (i, 0)),
                  pl.BlockSpec((1, d), lambda i: (0, 0))],
        out_specs=pl.BlockSpec((tm, d), lambda i: (i, 0)),
        out_shape=jax.ShapeDtypeStruct((m, d), F32),
        compiler_params=_cparams("arbitrary"),
        name="gated_residual",
    )(o, modr, res, g.reshape(1, d))


def _rope_kernel(x_ref, cos_ref, sin_ref, o_ref):
    x = x_ref[...]
    o_ref[...] = (x * cos_ref[...] + pltpu.roll(x, HEAD_DIM // 2, 1) * sin_ref[...]).astype(o_ref.dtype)


def rope_tables(seq):
    t = jnp.arange(seq)
    row = (t // GRID_W).astype(F32)
    col = (t % GRID_W).astype(F32)
    n_freq = HEAD_DIM // 4
    inv = ROPE_THETA ** (-jnp.arange(n_freq, dtype=F32) / n_freq)
    ang = jnp.concatenate([row[:, None] * inv, col[:, None] * inv], axis=-1)
    cos, sin = jnp.cos(ang), jnp.sin(ang)
    return jnp.concatenate([cos, cos], axis=-1), jnp.concatenate([-sin, sin], axis=-1)


def rope_qk(p, n_lat_rows, seq, n_cols, tm=512):
    cos, sin = rope_tables(seq)
    tiles_per_seq = seq // tm
    return pl.pallas_call(
        _rope_kernel,
        grid=(n_lat_rows // tm, n_cols // HEAD_DIM),
        in_specs=[pl.BlockSpec((tm, HEAD_DIM), lambda i, j: (i, j)),
                  pl.BlockSpec((tm, HEAD_DIM), lambda i, j: (i % tiles_per_seq, 0)),
                  pl.BlockSpec((tm, HEAD_DIM), lambda i, j: (i % tiles_per_seq, 0))],
        out_specs=pl.BlockSpec((tm, HEAD_DIM), lambda i, j: (i, j)),
        out_shape=jax.ShapeDtypeStruct((n_lat_rows, n_cols), BF16),
        compiler_params=_cparams("arbitrary", "arbitrary"),
        name="rope_qk",
    )(p, cos, sin)


def _attn_group(q, k, v, mask, sink_col):
    s = lax.dot_general(q, k, (((1,), (1,)), ((), ())), preferred_element_type=F32) * ATTN_SCALE
    if mask is not None:
        s = jnp.where(mask, s, -jnp.inf)
    m = jnp.maximum(jnp.max(s, axis=-1, keepdims=True), sink_col)
    e = jnp.exp(s - m)
    den = jnp.sum(e, axis=-1, keepdims=True) + jnp.exp(sink_col - m)
    o = jnp.dot(e.astype(BF16), v, preferred_element_type=F32)
    return o / den


def _sink_col(sink_ref, hk):
    return jnp.concatenate([jnp.full((BLOCK, 1), sink_ref[hk * ATTN_GROUP + g], F32)
                            for g in range(ATTN_GROUP)], axis=0)


def _attn_lat_kernel(sink_ref, q_ref, kp_ref, kc_ref, kn_ref, vp_ref, vc_ref, vn_ref, kx_ref, vx_ref,
                     o_ref, *, n_blocks, n_ctx):
    n = pl.program_id(1)
    row = lax.broadcasted_iota(jnp.int32, (BLOCK, BLOCK), 0)
    col = lax.broadcasted_iota(jnp.int32, (BLOCK, BLOCK), 1)
    m_prev = (col >= row) & (n > 0)
    m_next = (col <= row) & (n < n_blocks - 1)
    ones_ctx = jnp.ones((BLOCK, n_ctx), jnp.bool_)
    ones_cur = jnp.ones((BLOCK, BLOCK), jnp.bool_)
    mask1 = jnp.concatenate([ones_ctx, m_prev, ones_cur, m_next], axis=1)
    mask = jnp.concatenate([mask1] * ATTN_GROUP, axis=0)
    for hk in range(ATTN_KV_HEADS):
        hs = slice(hk * HEAD_DIM, (hk + 1) * HEAD_DIM)
        k = jnp.concatenate([kx_ref[:, hs].astype(BF16), kp_ref[:, hs], kc_ref[:, hs], kn_ref[:, hs]], axis=0)
        v = jnp.concatenate([vx_ref[:, hs], vp_ref[:, hs], vc_ref[:, hs], vn_ref[:, hs]], axis=0).astype(BF16)
        q = jnp.concatenate([q_ref[:, (hk * ATTN_GROUP + g) * HEAD_DIM:(hk * ATTN_GROUP + g + 1) * HEAD_DIM]
                             for g in range(ATTN_GROUP)], axis=0)
        o = _attn_group(q, k, v, mask, _sink_col(sink_ref, hk))
        for g in range(ATTN_GROUP):
            h = hk * ATTN_GROUP + g
            o_ref[:, h * HEAD_DIM:(h + 1) * HEAD_DIM] = o[g * BLOCK:(g + 1) * BLOCK].astype(o_ref.dtype)


def attention_latent(qk, p, sink, bsz, seq, n_ctx):
    nb = seq // BLOCK
    kvw = ATTN_KV_HEADS * HEAD_DIM
    qw = ATTN_HEADS * HEAD_DIM
    kcol, vcol = qw // kvw, (qw + kvw) // kvw
    ctx_blk0 = bsz * seq // n_ctx

    def prev(b, n):
        return b * nb + jnp.maximum(n - 1, 0)

    def nxt(b, n):
        return b * nb + jnp.minimum(n + 1, nb - 1)

    return pl.pallas_call(
        functools.partial(_attn_lat_kernel, n_blocks=nb, n_ctx=n_ctx),
        grid=(bsz, nb),
        in_specs=[pl.BlockSpec(memory_space=pltpu.SMEM),
                  pl.BlockSpec((BLOCK, qw), lambda b, n: (b * nb + n, 0)),
                  pl.BlockSpec((BLOCK, kvw), lambda b, n: (prev(b, n), kcol)),
                  pl.BlockSpec((BLOCK, kvw), lambda b, n: (b * nb + n, kcol)),
                  pl.BlockSpec((BLOCK, kvw), lambda b, n: (nxt(b, n), kcol)),
                  pl.BlockSpec((BLOCK, kvw), lambda b, n: (prev(b, n), vcol)),
                  pl.BlockSpec((BLOCK, kvw), lambda b, n: (b * nb + n, vcol)),
                  pl.BlockSpec((BLOCK, kvw), lambda b, n: (nxt(b, n), vcol)),
                  pl.BlockSpec((n_ctx, kvw), lambda b, n: (ctx_blk0 + b, kcol)),
                  pl.BlockSpec((n_ctx, kvw), lambda b, n: (ctx_blk0 + b, vcol))],
        out_specs=pl.BlockSpec((BLOCK, qw), lambda b, n: (b * nb + n, 0)),
        out_shape=jax.ShapeDtypeStruct((bsz * seq, qw), BF16),
        compiler_params=_cparams("arbitrary", "arbitrary"),
        name="attention_latent",
    )(sink, qk, qk, qk, qk, p, p, p, p, p)


def _attn_ctx_kernel(sink_ref, q_ref, k_ref, v_ref, o_ref, *, n_ctx):
    for hk in range(ATTN_KV_HEADS):
        hs = slice(hk * HEAD_DIM, (hk + 1) * HEAD_DIM)
        k = k_ref[:, hs].astype(BF16)
        v = v_ref[:, hs].astype(BF16)
        for g in range(ATTN_GROUP):
            h = hk * ATTN_GROUP + g
            q = q_ref[:, h * HEAD_DIM:(h + 1) * HEAD_DIM].astype(BF16)
            sink_col = jnp.full((n_ctx, 1), sink_ref[h], F32)
            o = _attn_group(q, k, v, None, sink_col)
            o_ref[:, h * HEAD_DIM:(h + 1) * HEAD_DIM] = o.astype(o_ref.dtype)


def attention_context(p, sink, bsz, seq, n_ctx):
    kvw = ATTN_KV_HEADS * HEAD_DIM
    qw = ATTN_HEADS * HEAD_DIM
    kcol, vcol = qw // kvw, (qw + kvw) // kvw
    ctx_blk0 = bsz * seq // n_ctx
    return pl.pallas_call(
        functools.partial(_attn_ctx_kernel, n_ctx=n_ctx),
        grid=(bsz,),
        in_specs=[pl.BlockSpec(memory_space=pltpu.SMEM),
                  pl.BlockSpec((n_ctx, qw), lambda b: (ctx_blk0 + b, 0)),
                  pl.BlockSpec((n_ctx, kvw), lambda b: (ctx_blk0 + b, kcol)),
                  pl.BlockSpec((n_ctx, kvw), lambda b: (ctx_blk0 + b, vcol))],
        out_specs=pl.BlockSpec((n_ctx, qw), lambda b: (b, 0)),
        out_shape=jax.ShapeDtypeStruct((bsz * n_ctx, qw), BF16),
        compiler_params=_cparams("arbitrary"),
        name="attention_context",
    )(sink, p, p, p)


def _cmul(ar, ai, br, bi):
    return ar * br - ai * bi, ar * bi + ai * br


def s5_group_matrices(a_re, a_im, log_dt, b_re, b_im, c_re, c_im, d):
    hp = lax.Precision.HIGHEST
    t = S5_T
    dt = jnp.exp(log_dt)
    mag = jnp.exp(dt * a_re)
    lr, li = mag * jnp.cos(dt * a_im), mag * jnp.sin(dt * a_im)
    den = a_re * a_re + a_im * a_im
    f_re = ((lr - 1) * a_re + li * a_im) / den
    f_im = (li * a_re - (lr - 1) * a_im) / den
    bb_re = f_re[..., None] * b_re - f_im[..., None] * b_im
    bb_im = f_re[..., None] * b_im + f_im[..., None] * b_re
    pr, pi = [jnp.ones_like(lr)], [jnp.zeros_like(li)]
    for _ in range(t):
        nr, ni = _cmul(pr[-1], pi[-1], lr, li)
        pr.append(nr)
        pi.append(ni)
    pw_re, pw_im = jnp.stack(pr, axis=1), jnp.stack(pi, axis=1)
    cl_re = c_re[:, None] * pw_re[:, :, :, None, :] - c_im[:, None] * pw_im[:, :, :, None, :]
    cl_im = c_re[:, None] * pw_im[:, :, :, None, :] + c_im[:, None] * pw_re[:, :, :, None, :]
    kern = (jnp.einsum('dkgcp,dgpe->dkgce', cl_re, bb_re, precision=hp)
            - jnp.einsum('dkgcp,dgpe->dkgce', cl_im, bb_im, precision=hp))
    g = a_re.shape[1]
    ii = jnp.arange(t)[:, None]
    jj = jnp.arange(t)[None, :]
    kf = kern[0][jnp.clip(jj - ii, 0, t)]
    kb = kern[1][jnp.clip(ii - jj, 0, t)]
    mf = (ii <= jj)[:, :, None, None, None]
    mb = (ii >= jj)[:, :, None, None, None]
    eye = (ii == jj)[:, :, None, None, None] * jnp.eye(S5_CH, dtype=F32)[None, None, None] * d[None, None, :, :, None]
    mm = jnp.where(mf, kf, 0.0) + jnp.where(mb, kb, 0.0) + eye
    m_mat = mm.transpose(2, 0, 4, 1, 3).reshape(g, t * S5_CH, t * S5_CH)
    pf = jnp.arange(t - 1, -1, -1)
    pb = jnp.arange(t)

    def drive(dirn, pidx):
        er, ei = _cmul(pw_re[dirn][pidx][:, :, :, None], pw_im[dirn][pidx][:, :, :, None],
                       bb_re[dirn][None], bb_im[dirn][None])
        er = er.transpose(1, 0, 3, 2).reshape(g, t * S5_CH, S5_STATE)
        ei = ei.transpose(1, 0, 3, 2).reshape(g, t * S5_CH, S5_STATE)
        return er, ei

    ef_re, ef_im = drive(0, pf)
    eb_re, eb_im = drive(1, pb)
    e_mat = jnp.concatenate([ef_re, eb_re, ef_im, eb_im], axis=-1)
    rf = jnp.arange(1, t + 1)
    rb = jnp.arange(t, 0, -1)

    def read(dirn, pidx):
        rr = cl_re[dirn][pidx]
        ri = cl_im[dirn][pidx]
        rr = rr.transpose(1, 3, 0, 2).reshape(g, S5_STATE, t * S5_CH)
        ri = ri.transpose(1, 3, 0, 2).reshape(g, S5_STATE, t * S5_CH)
        return rr, -ri

    rf_re, rf_im = read(0, rf)
    rb_re, rb_im = read(1, rb)
    r_mat = jnp.concatenate([rf_re, rb_re, rf_im, rb_im], axis=1)
    lam = jnp.stack([jnp.concatenate([pw_re[0, t], pw_re[1, t]], axis=-1),
                     jnp.concatenate([pw_im[0, t], pw_im[1, t]], axis=-1)], axis=1)
    gb, nb, p, w = S5_GB, g // S5_GB, S5_STATE, t * S5_CH
    eye = jnp.eye(gb, dtype=F32)
    m_blk = jnp.einsum('ogicjd,gh->oigcjhd', m_mat.reshape(nb, gb, t, S5_CH, t, S5_CH), eye)
    e_blk = jnp.einsum('ogicap,gh->oigcahp', e_mat.reshape(nb, gb, t, S5_CH, 4, p), eye)
    r_blk = jnp.einsum('ogapjc,gh->oahpjgc', r_mat.reshape(nb, gb, 4, p, t, S5_CH), eye)
    lam_blk = lam.reshape(nb, gb, 2, 2, p).transpose(0, 2, 3, 1, 4).reshape(nb, 2, 2 * gb * p)
    return (m_blk.reshape(nb, gb * w, gb * w).astype(BF16), e_blk.reshape(nb, gb * w, 4 * gb * p).astype(BF16),
            r_blk.reshape(nb, 4 * gb * p, gb * w).astype(BF16), lam_blk)


def _s5_kernel(u_ref, m_ref, e_ref, r_ref, lam_ref, y_ref, d_scr, sp_scr, *, n_ctx_chunks, n_chunks):
    u = u_ref[...]
    d_scr[...] = jnp.dot(u, e_ref[...], preferred_element_type=F32)
    sl = S5_GB * S5_STATE
    lam_re = jnp.broadcast_to(lam_ref[0:1, :], (SUBLANES, 2 * sl))
    lam_im = jnp.broadcast_to(lam_ref[1:2, :], (SUBLANES, 2 * sl))

    def rows(n):
        return pl.ds(pl.multiple_of(n * SUBLANES, SUBLANES), SUBLANES)

    def step(s, carry):
        s_re, s_im = carry
        nf = s
        nb = jnp.where(s < n_ctx_chunks, n_ctx_chunks - 1 - s, n_chunks - 1 - (s - n_ctx_chunks))
        sp_scr[rows(nf), 0:sl] = s_re[:, 0:sl]
        sp_scr[rows(nf), 2 * sl:3 * sl] = s_im[:, 0:sl]
        sp_scr[rows(nb), sl:2 * sl] = s_re[:, sl:2 * sl]
        sp_scr[rows(nb), 3 * sl:4 * sl] = s_im[:, sl:2 * sl]
        d_re = jnp.concatenate([d_scr[rows(nf), 0:sl], d_scr[rows(nb), sl:2 * sl]], axis=1)
        d_im = jnp.concatenate([d_scr[rows(nf), 2 * sl:3 * sl], d_scr[rows(nb), 3 * sl:4 * sl]], axis=1)
        n_re = lam_re * s_re - lam_im * s_im + d_re
        n_im = lam_re * s_im + lam_im * s_re + d_im
        return n_re, n_im

    zero = jnp.zeros((SUBLANES, 2 * sl), F32)
    lax.fori_loop(0, n_chunks, step, (zero, zero))
    y_ref[...] = (jnp.dot(u, m_ref[...], preferred_element_type=F32)
                  + jnp.dot(sp_scr[...].astype(BF16), r_ref[...], preferred_element_type=F32))


def s5_scan(u_g, m_mat, e_mat, r_mat, lam, n_ctx_chunks, n_chunks):
    nb, r, w = u_g.shape
    sw = e_mat.shape[2]
    return pl.pallas_call(
        functools.partial(_s5_kernel, n_ctx_chunks=n_ctx_chunks, n_chunks=n_chunks),
        grid=(nb,),
        in_specs=[pl.BlockSpec((None, r, w), lambda i: (i, 0, 0)),
                  pl.BlockSpec((None, w, w), lambda i: (i, 0, 0)),
                  pl.BlockSpec((None, w, sw), lambda i: (i, 0, 0)),
                  pl.BlockSpec((None, sw, w), lambda i: (i, 0, 0)),
                  pl.BlockSpec((None, 2, sw // 2), lambda i: (i, 0, 0))],
        out_specs=pl.BlockSpec((None, r, w), lambda i: (i, 0, 0)),
        out_shape=jax.ShapeDtypeStruct((nb, r, w), F32),
        scratch_shapes=[pltpu.VMEM((r, sw), F32), pltpu.VMEM((r, sw), F32)],
        compiler_params=_cparams("arbitrary"),
        name="s5_scan",
    )(u_g, m_mat, e_mat, r_mat, lam)


def s5_layer(u_all, bsz, seq, n_ctx, params):
    m_mat, e_mat, r_mat, lam = s5_group_matrices(*params)
    width = u_all.shape[1]
    bw = S5_GB * S5_CH
    nb = width // bw
    t = S5_T

    def to_groups(u, length):
        u = u.reshape(bsz, length // t, t, nb, bw).transpose(3, 1, 0, 2, 4)
        u = jnp.pad(u, ((0, 0), (0, 0), (0, SUBLANES - bsz), (0, 0), (0, 0)))
        return u.reshape(nb, length // t, SUBLANES, t * bw)

    n_lat = bsz * seq
    ncc, ncl = n_ctx // t, seq // t
    u_g = jnp.concatenate([to_groups(u_all[n_lat:], n_ctx), to_groups(u_all[:n_lat], seq)], axis=1)
    u_g = u_g.reshape(nb, (ncc + ncl) * SUBLANES, t * bw).astype(BF16)
    y_g = s5_scan(u_g, m_mat, e_mat, r_mat, lam, ncc, ncc + ncl)
    y_g = y_g.reshape(nb, ncc + ncl, SUBLANES, t, bw)[:, :, :bsz]

    def from_groups(y, length):
        return y.transpose(2, 1, 3, 0, 4).reshape(bsz * length, width)

    return jnp.concatenate([from_groups(y_g[:, ncc:], seq), from_groups(y_g[:, :ncc], n_ctx)], axis=0)


def _glu_kernel(y_ref, w_ref, b_ref, o_ref):
    gl = _gelu(y_ref[...])
    z = jnp.dot(gl.astype(BF16), w_ref[...], preferred_element_type=F32) + b_ref[...]
    o_ref[...] = (gl * jax.nn.sigmoid(z)).astype(o_ref.dtype)


def s5_glu(y, w, b, tm=512):
    m, n = y.shape
    return pl.pallas_call(
        _glu_kernel,
        grid=(m // tm,),
        in_specs=[pl.BlockSpec((tm, n), lambda i: (i, 0)),
                  pl.BlockSpec((n, n), lambda i: (0, 0)),
                  pl.BlockSpec((1, n), lambda i: (0, 0))],
        out_specs=pl.BlockSpec((tm, n), lambda i: (i, 0)),
        out_shape=jax.ShapeDtypeStruct((m, n), BF16),
        compiler_params=_cparams("arbitrary"),
        name="s5_glu",
    )(y, w, b.reshape(1, n))


def _top_rows(s, n_take, break_ties):
    r = s.shape[0]
    rows = lax.broadcasted_iota(jnp.int32, s.shape, 0).astype(F32) if break_ties else None
    rank = jnp.full(s.shape, NOT_SELECTED, F32)
    vals = []
    for k in range(n_take):
        m = jnp.max(s, axis=0, keepdims=True)
        if break_ties:
            idx = jnp.min(jnp.where(s == m, rows, float(r)), axis=0, keepdims=True)
            sel = rows == idx
        else:
            sel = s == m
        rank = jnp.where(sel, float(k), rank)
        s = jnp.where(sel, -jnp.inf, s)
        vals.append(m)
    return vals, rank, s


_CAND_WIDTHS = [PEER_TOPK // (i + 1) for i in range(PEER_TOPK)]
_CAND_STARTS = [sum(_CAND_WIDTHS[:i]) for i in range(PEER_TOPK + 1)]
_CAND_ROWS = -(-_CAND_STARTS[-1] // SUBLANES) * SUBLANES


def _route_head(s1, s2, grp, break_ties):
    v1, rank1, _ = _top_rows(s1, PEER_TOPK, break_ties)
    v2, rank2, _ = _top_rows(s2, PEER_TOPK, break_ties)
    v2m = jnp.concatenate(v2, axis=0)
    pad = jnp.full((_CAND_ROWS - _CAND_STARTS[-1], v2m.shape[1]), -jnp.inf, F32)
    cand = jnp.concatenate([v1[i] + v2m[0:_CAND_WIDTHS[i]] for i in range(PEER_TOPK)] + [pad], axis=0)
    cv, _, left = _top_rows(cand, PEER_TOPK, break_ties)
    picked = jnp.where(left != cand, 1.0, 0.0)
    picked = jnp.concatenate([picked, jnp.zeros((grp.shape[1] - _CAND_ROWS, picked.shape[1]), F32)], axis=0)
    cnt = jnp.dot(grp, picked, preferred_element_type=F32)
    z = jnp.zeros_like(cv[0])
    for k in range(PEER_TOPK):
        z = z + jnp.exp(cv[k] - cv[0])
    thr = jnp.zeros_like(s1)
    for i in range(PEER_TOPK):
        thr = jnp.where(rank1 == float(i), cnt[i:i + 1, :], thr)
    n_out = (jnp.sum(jnp.where(rank1 != NOT_SELECTED, 1.0, 0.0), axis=0, keepdims=True)
             + jnp.sum(jnp.where(rank2 != NOT_SELECTED, 1.0, 0.0), axis=0, keepdims=True)
             + jnp.sum(cnt, axis=0, keepdims=True))
    return thr, jnp.exp(s1 - v1[0]), rank2.astype(BF16), (jnp.exp(s2 - v2[0]) / z).astype(BF16), n_out


def _peer_route_kernel(q_ref, k1_ref, k2_ref, grp_ref, thr_ref, a_ref, r2_ref, b_ref):
    hp = lax.Precision.HIGHEST
    nt = (((1,), (1,)), ((), ()))
    for h in range(PEER_HEADS):
        qh = q_ref[:, h * 2 * PEER_HALF:(h + 1) * 2 * PEER_HALF]
        s1 = lax.dot_general(k1_ref[h], qh, nt, precision=hp, preferred_element_type=F32)
        s2 = lax.dot_general(k2_ref[h], qh, nt, precision=hp, preferred_element_type=F32)

        def write(res, h=h):
            thr_ref[h], a_ref[h], r2_ref[h], b_ref[h] = res[:4]

        fast = _route_head(s1, s2, grp_ref[...], break_ties=False)
        write(fast)

        @pl.when(jnp.max(fast[4]) > 3.0 * PEER_TOPK)
        def _(s1=s1, s2=s2, write=write):
            write(_route_head(s1, s2, grp_ref[...], break_ties=True))


def peer_route(q, keys, tt=256):
    t = q.shape[0]
    zeros = jnp.zeros((PEER_HEADS, PEER_NKEYS, PEER_HALF), F32)
    k1 = jnp.concatenate([keys[:, 0], zeros], axis=-1)
    k2 = jnp.concatenate([zeros, keys[:, 1]], axis=-1)
    row = jnp.arange(LANES)[None, :]
    grp = ((row >= jnp.array(_CAND_STARTS[:-1])[:, None]) & (row < jnp.array(_CAND_STARTS[1:])[:, None])).astype(F32)
    shp32 = jax.ShapeDtypeStruct((PEER_HEADS, PEER_NKEYS, t), F32)
    shp16 = jax.ShapeDtypeStruct((PEER_HEADS, PEER_NKEYS, t), BF16)
    ospec = pl.BlockSpec((PEER_HEADS, PEER_NKEYS, tt), lambda i: (0, 0, i))
    kspec = pl.BlockSpec((PEER_HEADS, PEER_NKEYS, 2 * PEER_HALF), lambda i: (0, 0, 0))
    return pl.pallas_call(
        _peer_route_kernel,
        grid=(t // tt,),
        in_specs=[pl.BlockSpec((tt, PEER_HEADS * 2 * PEER_HALF), lambda i: (i, 0)), kspec, kspec,
                  pl.BlockSpec((PEER_TOPK, LANES), lambda i: (0, 0))],
        out_specs=[ospec] * 4,
        out_shape=[shp32, shp32, shp16, shp16],
        compiler_params=_cparams("arbitrary"),
        name="peer_route",
    )(q, k1, k2, grp)


def _rows_bf16(row, n_rows):
    packed = jnp.broadcast_to(row, (2 * SUBLANES, row.shape[1])).astype(BF16)
    return jnp.concatenate([packed] * (n_rows // (2 * SUBLANES)), axis=0)


def _peer_dense_kernel(ft_ref, u_ref, vt_ref, thr_ref, a_ref, r2_ref, b_ref, o_ref, acc_scr, ge_scr, *,
                       rows_per_tile):
    j = pl.program_id(1)

    @pl.when(j == 0)
    def _():
        acc_scr[...] = jnp.zeros_like(acc_scr)
        ge_scr[...] = jnp.zeros_like(ge_scr)

    w_rows = []
    for r in range(rows_per_tile):
        w = None
        for h in range(PEER_HEADS):
            thr = _rows_bf16(thr_ref[h, r:r + 1, :], PEER_NKEYS)
            a = _rows_bf16(a_ref[h, r:r + 1, :], PEER_NKEYS)
            b = b_ref[h]
            term = jnp.where(r2_ref[h] < thr, b, jnp.zeros_like(b)) * a
            w = term if w is None else w + term
        w_rows.append(w * ge_scr[r * PEER_NKEYS:(r + 1) * PEER_NKEYS, :])
    wt = jnp.concatenate(w_rows, axis=0)
    acc_scr[...] += jnp.dot(vt_ref[...], wt, preferred_element_type=F32)
    act = jnp.dot(u_ref[...], ft_ref[...], preferred_element_type=F32)
    ge_scr[...] = _gelu(act).astype(BF16)

    @pl.when(j == pl.num_programs(1) - 1)
    def _():
        o_ref[...] = acc_scr[...].T


def peer_dense(ft, u, vt, thr, a, r2, b, tt=512, rows_per_tile=8):
    d, t = ft.shape
    e = u.shape[0]
    te = rows_per_tile * PEER_NKEYS
    ne = e // te

    def cur(j):
        return jnp.minimum(j, ne - 1)

    def prev(j):
        return jnp.maximum(j - 1, 0)

    rspec = pl.BlockSpec((PEER_HEADS, rows_per_tile, tt), lambda i, j: (0, prev(j), i))
    cspec = pl.BlockSpec((PEER_HEADS, PEER_NKEYS, tt), lambda i, j: (0, 0, i))
    return pl.pallas_call(
        functools.partial(_peer_dense_kernel, rows_per_tile=rows_per_tile),
        grid=(t // tt, ne + 1),
        in_specs=[pl.BlockSpec((d, tt), lambda i, j: (0, i)),
                  pl.BlockSpec((te, d), lambda i, j: (cur(j), 0)),
                  pl.BlockSpec((d, te), lambda i, j: (0, prev(j))),
                  rspec, rspec, cspec, cspec],
        out_specs=pl.BlockSpec((tt, d), lambda i, j: (i, 0)),
        out_shape=jax.ShapeDtypeStruct((t, d), F32),
        scratch_shapes=[pltpu.VMEM((d, tt), F32), pltpu.VMEM((te, tt), BF16)],
        compiler_params=_cparams("arbitrary", "arbitrary"),
        name="peer_dense",
    )(ft, u, vt, thr, a, r2, b)


def peer_ffn(h, modr, mod_row, norm_g, wq, keys, u, v, tm, final_norm_g=None):
    q, f = norm_mod_matmul(h, norm_g, modr, 3, 4, wq.astype(BF16), mod_row, tm=tm, tn=512, emit_a=True)
    thr, a, r2, b = peer_route(q, keys)
    out = peer_dense(f.T, u.astype(BF16), v.T.astype(BF16), thr, a, r2, b)
    return gated_residual(out, modr, 5, h, mod_row, tm=tm, norm_g=final_norm_g)


def _conv_kernel(x_ref, w_ref, b_ref, o_ref, pad_scr, *, length, row_tile):
    half = SSD_CONV // 2
    zeros = jnp.zeros((SUBLANES, x_ref.shape[1]), F32)
    pad_scr[0:SUBLANES, :] = zeros
    pad_scr[SUBLANES + length:2 * SUBLANES + length, :] = zeros
    pad_scr[SUBLANES:SUBLANES + length, :] = x_ref[...].astype(F32)
    for r0 in range(0, length, row_tile):
        acc = jnp.broadcast_to(b_ref[...], (row_tile, x_ref.shape[1]))
        for k in range(SSD_CONV):
            start = SUBLANES + r0 + k - half
            acc = acc + w_ref[k:k + 1, :] * pad_scr[start:start + row_tile, :]
        o_ref[r0:r0 + row_tile, :] = jax.nn.silu(acc).astype(o_ref.dtype)


def ssd_conv(p, col0, width, row0, n_seq, length, conv_w, conv_b, tc=256):
    row_tile = min(length, 512)
    return pl.pallas_call(
        functools.partial(_conv_kernel, length=length, row_tile=row_tile),
        grid=(n_seq, width // tc),
        in_specs=[pl.BlockSpec((length, tc), lambda s, j: (row0 // length + s, col0 // tc + j)),
                  pl.BlockSpec((SSD_CONV, tc), lambda s, j: (0, j)),
                  pl.BlockSpec((1, tc), lambda s, j: (0, j))],
        out_specs=pl.BlockSpec((length, tc), lambda s, j: (s, j)),
        out_shape=jax.ShapeDtypeStruct((n_seq * length, width), BF16),
        scratch_shapes=[pltpu.VMEM((length + 2 * SUBLANES, tc), F32)],
        compiler_params=_cparams("arbitrary", "arbitrary"),
        name="ssd_conv",
    )(p, conv_w, conv_b.reshape(1, width))


def _ssd_kernel(xl_ref, bl_ref, cl_ref, xx_ref, bx_ref, cx_ref, dt_ref, bias_ref, alog_ref, y_ref, st_scr, *,
                reverse, n_ctx_chunks):
    hp = lax.Precision.HIGHEST
    ch = SSD_CHUNK
    nh = dt_ref.shape[1]
    hpg = nh // SSD_GROUPS
    is_ctx = pl.program_id(1) < n_ctx_chunks

    def pick(ctx_ref, lat_ref, cols):
        return jnp.where(is_ctx, ctx_ref[:, cols], lat_ref[:, cols])

    @pl.when(pl.program_id(1) == 0)
    def _():
        st_scr[...] = jnp.zeros_like(st_scr)

    dt = jax.nn.softplus(dt_ref[...] + bias_ref[...])
    la = dt * (-jnp.exp(alog_ref[...]))
    row = lax.broadcasted_iota(jnp.int32, (ch, ch), 0)
    col = lax.broadcasted_iota(jnp.int32, (ch, ch), 1)
    keep = (col >= row) if reverse else (row >= col)
    tri = jnp.where(keep, 1.0, 0.0)
    cs = jnp.dot(tri, la, precision=hp, preferred_element_type=F32)
    both_t = jnp.concatenate([cs, dt], axis=1).T
    cs_t, dt_t = both_t[0:nh], both_t[nh:2 * nh]
    end = 0 if reverse else ch - 1
    tot_t = cs_t[:, end:end + 1]
    ws_t = jnp.exp(tot_t - cs_t) * dt_t
    dec_t = jnp.exp(tot_t)
    ecs = jnp.exp(cs)
    lo = lax.broadcasted_iota(jnp.int32, (ch, 2 * SSD_HEAD_DIM), 1) < SSD_HEAD_DIM
    nt = (((1,), (1,)), ((), ()))
    for g in range(SSD_GROUPS):
        gs = slice(g * SSD_STATE, (g + 1) * SSD_STATE)
        cg = pick(cx_ref, cl_ref, gs)
        bg = pick(bx_ref, bl_ref, gs)
        cb = lax.dot_general(cg, bg, nt, preferred_element_type=F32)
        bg_t = bg.astype(F32).T
        cg32 = cg.astype(F32)
        for pr in range(hpg // 2):
            lanes = slice((g * hpg + 2 * pr) * SSD_HEAD_DIM, (g * hpg + 2 * pr + 2) * SSD_HEAD_DIM)
            sl = slice(2 * pr * SSD_HEAD_DIM, (2 * pr + 2) * SSD_HEAD_DIM)
            xp = pick(xx_ref, xl_ref, lanes)
            st = st_scr[g, :, sl]
            rhs = jnp.concatenate([xp, st.astype(BF16)], axis=0)
            ys, news, decs = [], [], []
            for j in range(2):
                h = g * hpg + 2 * pr + j
                seg = cs[:, h:h + 1] - cs_t[h:h + 1, :]
                gmat = jnp.where(keep, jnp.exp(seg), 0.0) * cb * dt_t[h:h + 1, :]
                cmat = cg32 * ecs[:, h:h + 1]
                lhs = jnp.concatenate([gmat, cmat], axis=1).astype(BF16)
                ys.append(jnp.dot(lhs, rhs, preferred_element_type=F32))
                bw = (bg_t * ws_t[h:h + 1, :]).astype(BF16)
                news.append(jnp.dot(bw, xp, preferred_element_type=F32))
                decs.append(jnp.broadcast_to(dec_t[h:h + 1, :], (ch, 2 * SSD_HEAD_DIM)))
            y_ref[:, lanes] = jnp.where(lo, ys[0], ys[1]).astype(y_ref.dtype)
            st_scr[g, :, sl] = st * jnp.where(lo, decs[0], decs[1]) + jnp.where(lo, news[0], news[1])


def ssd_scan(xc_lat, xc_ctx, dt_raw, dt_bias, a_log, bsz, seq, n_ctx, reverse):
    nh = dt_raw.shape[1]
    inner = nh * SSD_HEAD_DIM
    gn = SSD_GROUPS * SSD_STATE
    nl, nc = seq // SSD_CHUNK, n_ctx // SSD_CHUNK
    lat_blocks = bsz * nl

    def lblk(b, s):
        sl = jnp.maximum(s - nc, 0)
        return b * nl + ((nl - 1 - sl) if reverse else sl)

    def cblk(b, s):
        sc = jnp.minimum(s, nc - 1)
        return b * nc + ((nc - 1 - sc) if reverse else sc)

    def blk(b, s):
        return jnp.where(s < nc, lat_blocks + cblk(b, s), lblk(b, s))

    def xbc_specs(fn):
        return [pl.BlockSpec((SSD_CHUNK, inner), lambda b, s: (fn(b, s), 0)),
                pl.BlockSpec((SSD_CHUNK, gn), lambda b, s: (fn(b, s), inner // gn)),
                pl.BlockSpec((SSD_CHUNK, gn), lambda b, s: (fn(b, s), inner // gn + 1))]

    return pl.pallas_call(
        functools.partial(_ssd_kernel, reverse=reverse, n_ctx_chunks=nc),
        grid=(bsz, nc + nl),
        in_specs=xbc_specs(lblk) + xbc_specs(cblk) + [
            pl.BlockSpec((SSD_CHUNK, nh), lambda b, s: (blk(b, s), 0)),
            pl.BlockSpec((1, nh), lambda b, s: (0, 0)),
            pl.BlockSpec((1, nh), lambda b, s: (0, 0))],
        out_specs=pl.BlockSpec((SSD_CHUNK, inner), lambda b, s: (lblk(b, s), 0)),
        out_shape=jax.ShapeDtypeStruct((bsz * seq, inner), BF16),
        scratch_shapes=[pltpu.VMEM((SSD_GROUPS, SSD_STATE, inner // SSD_GROUPS), F32)],
        compiler_params=_cparams("arbitrary", "arbitrary"),
        name="ssd_scan_bwd" if reverse else "ssd_scan_fwd",
    )(xc_lat, xc_lat, xc_lat, xc_ctx, xc_ctx, xc_ctx, dt_raw, dt_bias.reshape(1, nh), a_log.reshape(1, nh))


def _ssd_out_kernel(yf_ref, yb_ref, x_ref, z_ref, d_ref, ng_ref, w_ref, gate_ref, res_ref, o_ref, a_scr):
    @pl.when(pl.program_id(1) == 0)
    def _():
        y = (x_ref[...].astype(F32) * d_ref[...] + yf_ref[...].astype(F32) + yb_ref[...].astype(F32))
        gv = y * jax.nn.silu(z_ref[...].astype(F32))
        gw = gv.shape[1] // SSD_GROUPS
        for g in range(SSD_GROUPS):
            part = gv[:, g * gw:(g + 1) * gw]
            ms = jnp.mean(part * part, axis=-1, keepdims=True)
            a_scr[:, g * gw:(g + 1) * gw] = (part * lax.rsqrt(ms + EPS) * ng_ref[:, g * gw:(g + 1) * gw]).astype(BF16)

    acc = jnp.dot(a_scr[...], w_ref[...], preferred_element_type=F32)
    o_ref[...] = res_ref[...] + gate_ref[...] * acc


def ssd_output(yf, yb, xc, pz, d_exp, norm_g, w_out, modr, res, mod_row, tm=512, tn=512):
    m, inner = yf.shape
    n = w_out.shape[1]
    return pl.pallas_call(
        _ssd_out_kernel,
        grid=(m // tm, n // tn),
        in_specs=[pl.BlockSpec((tm, inner), lambda i, j: (i, 0)),
                  pl.BlockSpec((tm, inner), lambda i, j: (i, 0)),
                  pl.BlockSpec((tm, inner), lambda i, j: (i, 0)),
                  pl.BlockSpec((tm, inner), lambda i, j: (i, 0)),
                  pl.BlockSpec((1, inner), lambda i, j: (0, 0)),
                  pl.BlockSpec((1, inner), lambda i, j: (0, 0)),
                  pl.BlockSpec((inner, tn), lambda i, j: (0, j)),
                  pl.BlockSpec((None, 1, tn), lambda i, j: (mod_row(i * tm) * N_MOD + 2, 0, j)),
                  pl.BlockSpec((tm, tn), lambda i, j: (i, j))],
        out_specs=pl.BlockSpec((tm, tn), lambda i, j: (i, j)),
        out_shape=jax.ShapeDtypeStruct((m, n), F32),
        scratch_shapes=[pltpu.VMEM((tm, inner), BF16)],
        compiler_params=_cparams("arbitrary", "arbitrary"),
        name="ssd_output",
    )(yf, yb, xc, pz, d_exp.reshape(1, inner), norm_g.reshape(1, inner), w_out, modr, res)


def _mod_rows(c, c_ctx, w, b):
    bsz, d = c.shape
    cond = jnp.zeros((SUBLANES, d), F32).at[:bsz].set(c).at[bsz].set(c_ctx)
    mod = modulation(cond, w, b)
    return mod.reshape(SUBLANES * N_MOD, 1, d)


def attn_s5_layer(h, bsz, seq, n_ctx, modr, mod_row, tm, norm1_g, w_in, sink, s5_params, glu_w, glu_b, w_out):
    n_lat = bsz * seq
    qw = ATTN_HEADS * HEAD_DIM
    kvw = ATTN_KV_HEADS * HEAD_DIM
    p = norm_mod_matmul(h, norm1_g, modr, 0, 1, w_in.astype(BF16), mod_row, tm=tm, tn=512)
    qk = rope_qk(p, n_lat, seq, qw + kvw)
    attn_lat = attention_latent(qk, p, sink, bsz, seq, n_ctx)
    attn_ctx = attention_context(p, sink, bsz, seq, n_ctx)
    attn = jnp.concatenate([attn_lat, attn_ctx], axis=0)
    y = s5_layer(p[:, qw + 2 * kvw:], bsz, seq, n_ctx, s5_params)
    s5o = s5_glu(y, glu_w.astype(BF16), glu_b)
    w_out = w_out.astype(BF16)
    return matmul_gated_residual([attn, s5o], [w_out[:qw], w_out[qw:]], modr, 2, h, mod_row, tm=tm, tn=512)


def ssd_layer(h, bsz, seq, n_ctx, modr, mod_row, tm, norm1_g, w_in, conv_w, conv_b, dt_bias, a_log, d_skip,
              norm_g, w_out):
    n_lat = bsz * seq
    nh = a_log.shape[1]
    inner = nh * SSD_HEAD_DIM
    xbc_w = inner + 2 * SSD_GROUPS * SSD_STATE
    w_in = w_in.astype(BF16)
    pz = norm_mod_matmul(h, norm1_g, modr, 0, 1, w_in[:, :inner + xbc_w], mod_row, tm=tm, tn=1024,
                         out_dtype=BF16)
    dt_raw = norm_mod_matmul(h, norm1_g, modr, 0, 1, w_in[:, inner + xbc_w:], mod_row, tm=tm, tn=2 * nh)
    xc = ssd_conv(pz, inner, xbc_w, 0, bsz, seq, conv_w, conv_b)
    xc_ctx = ssd_conv(pz, inner, xbc_w, n_lat, bsz, n_ctx, conv_w, conv_b)
    yf = ssd_scan(xc, xc_ctx, dt_raw[:, :nh], dt_bias[0], a_log[0], bsz, seq, n_ctx, reverse=False)
    yb = ssd_scan(xc, xc_ctx, dt_raw[:, nh:], dt_bias[1], a_log[1], bsz, seq, n_ctx, reverse=True)
    d_exp = jnp.repeat(d_skip, SSD_HEAD_DIM)
    return ssd_output(yf, yb, xc, pz, d_exp, norm_g, w_out.astype(BF16), modr, h[:n_lat], mod_row)


def kernel(x, c, ctx, c_ctx, mod_w, mod_b, norm1_g, norm2_g, ab_w_in, attn_sink, s5_a_re, s5_a_im, s5_log_dt, s5_b_re, s5_b_im, s5_c_re, s5_c_im, s5_d, s5_glu_w, s5_glu_b, ab_w_out, ssd_w_in, ssd_conv_w, ssd_conv_b, ssd_dt_bias, ssd_a_log, ssd_d, ssd_norm_g, ssd_w_out, peer_wq, peer_keys, peer_u, peer_v, final_norm_g):
    bsz, seq, d = x.shape
    n_ctx = ctx.shape[1]
    n_lat = bsz * seq
    depth = mod_w.shape[0]
    tm = 512

    def mod_row(r):
        return jnp.minimum(r // seq, bsz)

    h = jnp.concatenate([x.reshape(n_lat, d), ctx.reshape(bsz * n_ctx, d)], axis=0)
    for i in range(depth):
        last = i == depth - 1
        j = i // 2
        modr = _mod_rows(c, c_ctx, mod_w[i], mod_b[i])
        if i % 2 == 0:
            s5_params = (s5_a_re[j], s5_a_im[j], s5_log_dt[j], s5_b_re[j], s5_b_im[j], s5_c_re[j], s5_c_im[j],
                         s5_d[j])
            h = attn_s5_layer(h, bsz, seq, n_ctx, modr, mod_row, tm, norm1_g[i], ab_w_in[j], attn_sink[j],
                              s5_params, s5_glu_w[j], s5_glu_b[j], ab_w_out[j])
        else:
            if not last:
                raise NotImplementedError("an SSD layer that must also update the context stream")
            h = ssd_layer(h, bsz, seq, n_ctx, modr, mod_row, tm, norm1_g[i], ssd_w_in[j], ssd_conv_w[j],
                          ssd_conv_b[j], ssd_dt_bias[j], ssd_a_log[j], ssd_d[j], ssd_norm_g[j], ssd_w_out[j])
        h = peer_ffn(h, modr, mod_row, norm2_g[i], peer_wq[i], peer_keys[i], peer_u[i], peer_v[i], tm,
                     final_norm_g=final_norm_g if last else None)
    return h[:n_lat].reshape(bsz, seq, d)
```

```python
import functools
import math

import jax
import jax.numpy as jnp
from jax import lax
from jax.experimental import pallas as pl
from jax.experimental.pallas import tpu as pltpu

F32 = jnp.float32
BF16 = jnp.bfloat16
EPS = 1e-6

V7X_VMEM_LIMIT_BYTES = 56 * 1024 * 1024
LANES = 128
SUBLANES = 8

N_MOD = 6
HEAD_DIM = 128
ATTN_HEADS = 8
ATTN_KV_HEADS = 2
ATTN_GROUP = ATTN_HEADS // ATTN_KV_HEADS
ATTN_SCALE = HEAD_DIM ** -0.5
BLOCK = 128
GRID_W = 64
ROPE_THETA = 10000.0
S5_CH = 16
S5_STATE = 64
S5_T = 16
S5_GB = 2
SSD_HEAD_DIM = 64
SSD_GROUPS = 8
SSD_STATE = 128
SSD_CONV = 5
SSD_CHUNK = 128
PEER_HEADS = 8
PEER_NKEYS = 128
PEER_HALF = 64
PEER_TOPK = 16
NOT_SELECTED = 127.0


def _cparams(*sem):
    return pltpu.CompilerParams(dimension_semantics=sem, vmem_limit_bytes=V7X_VMEM_LIMIT_BYTES)


def _gelu(x):
    return 0.5 * x * (1.0 + lax.erf(x * (2.0 ** -0.5)))


def _mod_kernel(c_ref, w_ref, b_ref, o_ref):
    s = jax.nn.silu(c_ref[...]).astype(BF16)
    o_ref[...] = jnp.dot(s, w_ref[...].astype(BF16), preferred_element_type=F32) + b_ref[...]


def modulation(cond, w, b, layer, tn=512):
    rows, d = cond.shape
    n = w.shape[2]
    return pl.pallas_call(
        _mod_kernel,
        grid=(n // tn,),
        in_specs=[pl.BlockSpec((rows, d), lambda j: (0, 0)),
                  pl.BlockSpec((None, d, tn), lambda j: (layer, 0, j)),
                  pl.BlockSpec((None, 1, tn), lambda j: (layer, 0, j))],
        out_specs=pl.BlockSpec((rows, tn), lambda j: (0, j)),
        out_shape=jax.ShapeDtypeStruct((rows, n), F32),
        compiler_params=_cparams("arbitrary"),
        name="modulation",
    )(cond, w, b.reshape(b.shape[0], 1, n))


def _nmm_kernel(x_ref, g_ref, sh_ref, sc_ref, w_ref, *refs, emit_a):
    if emit_a:
        o_ref, a_out_ref, a_scr = refs
    else:
        o_ref, a_scr = refs

    @pl.when(pl.program_id(1) == 0)
    def _():
        x = x_ref[...]
        ms = jnp.mean(x * x, axis=-1, keepdims=True)
        y = x * lax.rsqrt(ms + EPS) * g_ref[...]
        a = y * (1.0 + sc_ref[...]) + sh_ref[...]
        a_scr[...] = a.astype(BF16)
        if emit_a:
            a_out_ref[...] = a.T.astype(BF16)

    o_ref[...] = jnp.dot(a_scr[...], w_ref[...], preferred_element_type=F32).astype(o_ref.dtype)


def norm_mod_matmul(x, g, modr, shift_idx, scale_idx, w, mod_row, tm, tn, out_dtype=F32, emit_a=False):
    m, d = x.shape
    n = w.shape[1]
    out_shape = [jax.ShapeDtypeStruct((m, n), out_dtype)]
    out_specs = [pl.BlockSpec((tm, tn), lambda i, j: (i, j))]
    if emit_a:
        out_shape.append(jax.ShapeDtypeStruct((d, m), BF16))
        out_specs.append(pl.BlockSpec((d, tm), lambda i, j: (0, i)))
    res = pl.pallas_call(
        functools.partial(_nmm_kernel, emit_a=emit_a),
        grid=(m // tm, n // tn),
        in_specs=[pl.BlockSpec((tm, d), lambda i, j: (i, 0)),
                  pl.BlockSpec((1, d), lambda i, j: (0, 0)),
                  pl.BlockSpec((None, 1, d), lambda i, j: (mod_row(i * tm) * N_MOD + shift_idx, 0, 0)),
                  pl.BlockSpec((None, 1, d), lambda i, j: (mod_row(i * tm) * N_MOD + scale_idx, 0, 0)),
                  pl.BlockSpec((d, tn), lambda i, j: (0, j))],
        out_specs=out_specs,
        out_shape=out_shape,
        scratch_shapes=[pltpu.VMEM((tm, d), BF16)],
        compiler_params=_cparams("arbitrary", "arbitrary"),
        name="norm_mod_matmul",
    )(x, g.reshape(1, d), modr, modr, w)
    return res if emit_a else res[0]


def _mmres_kernel(*refs, n_parts):
    xs = refs[:n_parts]
    ws = refs[n_parts:2 * n_parts]
    gate_ref, res_ref, o_ref = refs[2 * n_parts:]
    acc = jnp.dot(xs[0][...], ws[0][...], preferred_element_type=F32)
    for p in range(1, n_parts):
        acc = acc + jnp.dot(xs[p][...], ws[p][...], preferred_element_type=F32)
    o_ref[...] = res_ref[...] + gate_ref[...] * acc


def matmul_gated_residual(xs, ws, modr, gate_idx, res, mod_row, tm, tn):
    m, n = res.shape
    n_parts = len(xs)
    in_specs = ([pl.BlockSpec((tm, x.shape[1]), lambda i, j: (i, 0)) for x in xs]
                + [pl.BlockSpec((w.shape[0], tn), lambda i, j: (0, j)) for w in ws]
                + [pl.BlockSpec((None, 1, tn), lambda i, j: (mod_row(i * tm) * N_MOD + gate_idx, 0, j)),
                   pl.BlockSpec((tm, tn), lambda i, j: (i, j))])
    return pl.pallas_call(
        functools.partial(_mmres_kernel, n_parts=n_parts),
        grid=(m // tm, n // tn),
        in_specs=in_specs,
        out_specs=pl.BlockSpec((tm, tn), lambda i, j: (i, j)),
        out_shape=jax.ShapeDtypeStruct((m, n), F32),
        compiler_params=_cparams("arbitrary", "arbitrary"),
        name="matmul_gated_residual",
    )(*xs, *ws, modr, res)


def _rope_kernel(x_ref, cos_ref, sin_ref, o_ref):
    x = x_ref[...].astype(F32)
    o_ref[...] = (x * cos_ref[...] + pltpu.roll(x, HEAD_DIM // 2, 1) * sin_ref[...]).astype(o_ref.dtype)


def rope_tables(seq):
    t = jnp.arange(seq)
    row = (t // GRID_W).astype(F32)
    col = (t % GRID_W).astype(F32)
    n_freq = HEAD_DIM // 4
    inv = ROPE_THETA ** (-jnp.arange(n_freq, dtype=F32) / n_freq)
    ang = jnp.concatenate([row[:, None] * inv, col[:, None] * inv], axis=-1)
    cos, sin = jnp.cos(ang), jnp.sin(ang)
    return jnp.concatenate([cos, cos], axis=-1), jnp.concatenate([-sin, sin], axis=-1)


def rope_qk(p, n_lat_rows, seq, n_cols, tm=512):
    cos, sin = rope_tables(seq)
    tiles_per_seq = seq // tm
    return pl.pallas_call(
        _rope_kernel,
        grid=(n_lat_rows // tm, n_cols // HEAD_DIM),
        in_specs=[pl.BlockSpec((tm, HEAD_DIM), lambda i, j: (i, j)),
                  pl.BlockSpec((tm, HEAD_DIM), lambda i, j: (i % tiles_per_seq, 0)),
                  pl.BlockSpec((tm, HEAD_DIM), lambda i, j: (i % tiles_per_seq, 0))],
        out_specs=pl.BlockSpec((tm, HEAD_DIM), lambda i, j: (i, j)),
        out_shape=jax.ShapeDtypeStruct((n_lat_rows, n_cols), BF16),
        compiler_params=_cparams("arbitrary", "arbitrary"),
        name="rope_qk",
    )(p, cos, sin)


def _attn_group(q, k, v, mask, sink_col):
    s = lax.dot_general(q, k, (((1,), (1,)), ((), ())), preferred_element_type=F32) * ATTN_SCALE
    if mask is not None:
        s = jnp.where(mask, s, -jnp.inf)
    m = jnp.maximum(jnp.max(s, axis=-1, keepdims=True), sink_col)
    e = jnp.exp(s - m)
    den = jnp.sum(e, axis=-1, keepdims=True) + jnp.exp(sink_col - m)
    o = jnp.dot(e.astype(BF16), v, preferred_element_type=F32)
    return o / den


def _sink_col(sink_ref, hk):
    return jnp.concatenate([jnp.full((BLOCK, 1), sink_ref[hk * ATTN_GROUP + g], F32)
                            for g in range(ATTN_GROUP)], axis=0)


def _attn_lat_kernel(sink_ref, q_ref, kp_ref, kc_ref, kn_ref, vp_ref, vc_ref, vn_ref, kx_ref, vx_ref,
                     o_ref, *, n_blocks, n_ctx):
    n = pl.program_id(1)
    row = lax.broadcasted_iota(jnp.int32, (BLOCK, BLOCK), 0)
    col = lax.broadcasted_iota(jnp.int32, (BLOCK, BLOCK), 1)
    m_prev = (col >= row) & (n > 0)
    m_next = (col <= row) & (n < n_blocks - 1)
    ones_ctx = jnp.ones((BLOCK, n_ctx), jnp.bool_)
    ones_cur = jnp.ones((BLOCK, BLOCK), jnp.bool_)
    mask1 = jnp.concatenate([ones_ctx, m_prev, ones_cur, m_next], axis=1)
    mask = jnp.concatenate([mask1] * ATTN_GROUP, axis=0)
    for hk in range(ATTN_KV_HEADS):
        hs = slice(hk * HEAD_DIM, (hk + 1) * HEAD_DIM)
        k = jnp.concatenate([kx_ref[:, hs].astype(BF16), kp_ref[:, hs], kc_ref[:, hs], kn_ref[:, hs]], axis=0)
        v = jnp.concatenate([vx_ref[:, hs], vp_ref[:, hs], vc_ref[:, hs], vn_ref[:, hs]], axis=0).astype(BF16)
        q = jnp.concatenate([q_ref[:, (hk * ATTN_GROUP + g) * HEAD_DIM:(hk * ATTN_GROUP + g + 1) * HEAD_DIM]
                             for g in range(ATTN_GROUP)], axis=0)
        o = _attn_group(q, k, v, mask, _sink_col(sink_ref, hk))
        for g in range(ATTN_GROUP):
            h = hk * ATTN_GROUP + g
            o_ref[:, h * HEAD_DIM:(h + 1) * HEAD_DIM] = o[g * BLOCK:(g + 1) * BLOCK].astype(o_ref.dtype)


def attention_latent(qk, p, sink, bsz, seq, n_ctx):
    nb = seq // BLOCK
    kvw = ATTN_KV_HEADS * HEAD_DIM
    qw = ATTN_HEADS * HEAD_DIM
    kcol, vcol = qw // kvw, (qw + kvw) // kvw
    ctx_blk0 = bsz * seq // n_ctx

    def prev(b, n):
        return b * nb + jnp.maximum(n - 1, 0)

    def nxt(b, n):
        return b * nb + jnp.minimum(n + 1, nb - 1)

    return pl.pallas_call(
        functools.partial(_attn_lat_kernel, n_blocks=nb, n_ctx=n_ctx),
        grid=(bsz, nb),
        in_specs=[pl.BlockSpec(memory_space=pltpu.SMEM),
                  pl.BlockSpec((BLOCK, qw), lambda b, n: (b * nb + n, 0)),
                  pl.BlockSpec((BLOCK, kvw), lambda b, n: (prev(b, n), kcol)),
                  pl.BlockSpec((BLOCK, kvw), lambda b, n: (b * nb + n, kcol)),
                  pl.BlockSpec((BLOCK, kvw), lambda b, n: (nxt(b, n), kcol)),
                  pl.BlockSpec((BLOCK, kvw), lambda b, n: (prev(b, n), vcol)),
                  pl.BlockSpec((BLOCK, kvw), lambda b, n: (b * nb + n, vcol)),
                  pl.BlockSpec((BLOCK, kvw), lambda b, n: (nxt(b, n), vcol)),
                  pl.BlockSpec((n_ctx, kvw), lambda b, n: (ctx_blk0 + b, kcol)),
                  pl.BlockSpec((n_ctx, kvw), lambda b, n: (ctx_blk0 + b, vcol))],
        out_specs=pl.BlockSpec((BLOCK, qw), lambda b, n: (b * nb + n, 0)),
        out_shape=jax.ShapeDtypeStruct((bsz * seq, qw), BF16),
        compiler_params=_cparams("arbitrary", "arbitrary"),
        name="attention_latent",
    )(sink, qk, qk, qk, qk, p, p, p, p, p)


def _attn_ctx_kernel(sink_ref, q_ref, k_ref, v_ref, o_ref, *, n_ctx):
    for hk in range(ATTN_KV_HEADS):
        hs = slice(hk * HEAD_DIM, (hk + 1) * HEAD_DIM)
        k = k_ref[:, hs].astype(BF16)
        v = v_ref[:, hs].astype(BF16)
        for g in range(ATTN_GROUP):
            h = hk * ATTN_GROUP + g
            q = q_ref[:, h * HEAD_DIM:(h + 1) * HEAD_DIM].astype(BF16)
            sink_col = jnp.full((n_ctx, 1), sink_ref[h], F32)
            o = _attn_group(q, k, v, None, sink_col)
            o_ref[:, h * HEAD_DIM:(h + 1) * HEAD_DIM] = o.astype(o_ref.dtype)


def attention_context(p, sink, bsz, seq, n_ctx):
    kvw = ATTN_KV_HEADS * HEAD_DIM
    qw = ATTN_HEADS * HEAD_DIM
    kcol, vcol = qw // kvw, (qw + kvw) // kvw
    ctx_blk0 = bsz * seq // n_ctx
    return pl.pallas_call(
        functools.partial(_attn_ctx_kernel, n_ctx=n_ctx),
        grid=(bsz,),
        in_specs=[pl.BlockSpec(memory_space=pltpu.SMEM),
                  pl.BlockSpec((n_ctx, qw), lambda b: (ctx_blk0 + b, 0)),
                  pl.BlockSpec((n_ctx, kvw), lambda b: (ctx_blk0 + b, kcol)),
                  pl.BlockSpec((n_ctx, kvw), lambda b: (ctx_blk0 + b, vcol))],
        out_specs=pl.BlockSpec((n_ctx, qw), lambda b: (b, 0)),
        out_shape=jax.ShapeDtypeStruct((bsz * n_ctx, qw), BF16),
        compiler_params=_cparams("arbitrary"),
        name="attention_context",
    )(sink, p, p, p)


def _cmul(ar, ai, br, bi):
    return ar * br - ai * bi, ar * bi + ai * br


def s5_group_matrices(a_re, a_im, log_dt, b_re, b_im, c_re, c_im, d):
    hp = lax.Precision.HIGHEST
    t = S5_T
    dt = jnp.exp(log_dt)
    mag = jnp.exp(dt * a_re)
    lr, li = mag * jnp.cos(dt * a_im), mag * jnp.sin(dt * a_im)
    den = a_re * a_re + a_im * a_im
    f_re = ((lr - 1) * a_re + li * a_im) / den
    f_im = (li * a_re - (lr - 1) * a_im) / den
    bb_re = f_re[..., None] * b_re - f_im[..., None] * b_im
    bb_im = f_re[..., None] * b_im + f_im[..., None] * b_re
    pr, pi = [jnp.ones_like(lr)], [jnp.zeros_like(li)]
    for _ in range(t):
        nr, ni = _cmul(pr[-1], pi[-1], lr, li)
        pr.append(nr)
        pi.append(ni)
    pw_re, pw_im = jnp.stack(pr, axis=1), jnp.stack(pi, axis=1)
    cl_re = c_re[:, None] * pw_re[:, :, :, None, :] - c_im[:, None] * pw_im[:, :, :, None, :]
    cl_im = c_re[:, None] * pw_im[:, :, :, None, :] + c_im[:, None] * pw_re[:, :, :, None, :]
    kern = jnp.einsum('dkgcp,dgpe->dkgce', jnp.concatenate([cl_re, -cl_im], axis=-1),
                      jnp.concatenate([bb_re, bb_im], axis=2), precision=hp)
    g = a_re.shape[1]
    ii = jnp.arange(t)[:, None]
    jj = jnp.arange(t)[None, :]
    kf = kern[0][jnp.clip(jj - ii, 0, t)]
    kb = kern[1][jnp.clip(ii - jj, 0, t)]
    mf = (ii <= jj)[:, :, None, None, None]
    mb = (ii >= jj)[:, :, None, None, None]
    eye = (ii == jj)[:, :, None, None, None] * jnp.eye(S5_CH, dtype=F32)[None, None, None] * d[None, None, :, :, None]
    mm = jnp.where(mf, kf, 0.0) + jnp.where(mb, kb, 0.0) + eye
    m_mat = mm.transpose(2, 0, 4, 1, 3).reshape(g, t * S5_CH, t * S5_CH)
    pf = jnp.arange(t - 1, -1, -1)
    pb = jnp.arange(t)

    def drive(dirn, pidx):
        er, ei = _cmul(pw_re[dirn][pidx][:, :, :, None], pw_im[dirn][pidx][:, :, :, None],
                       bb_re[dirn][None], bb_im[dirn][None])
        er = er.transpose(1, 0, 3, 2).reshape(g, t * S5_CH, S5_STATE)
        ei = ei.transpose(1, 0, 3, 2).reshape(g, t * S5_CH, S5_STATE)
        return er, ei

    ef_re, ef_im = drive(0, pf)
    eb_re, eb_im = drive(1, pb)
    e_mat = jnp.concatenate([ef_re, eb_re, ef_im, eb_im], axis=-1)
    rf = jnp.arange(1, t + 1)
    rb = jnp.arange(t, 0, -1)

    def read(dirn, pidx):
        rr = cl_re[dirn][pidx]
        ri = cl_im[dirn][pidx]
        rr = rr.transpose(1, 3, 0, 2).reshape(g, S5_STATE, t * S5_CH)
        ri = ri.transpose(1, 3, 0, 2).reshape(g, S5_STATE, t * S5_CH)
        return rr, -ri

    rf_re, rf_im = read(0, rf)
    rb_re, rb_im = read(1, rb)
    r_mat = jnp.concatenate([rf_re, rb_re, rf_im, rb_im], axis=1)
    lam = jnp.stack([jnp.concatenate([pw_re[0, t], pw_re[1, t]], axis=-1),
                     jnp.concatenate([pw_im[0, t], pw_im[1, t]], axis=-1)], axis=1)
    gb, nb, p, w = S5_GB, g // S5_GB, S5_STATE, t * S5_CH
    grp = jnp.arange(gb)[:, None, None]
    src = jnp.arange(w)[None, :, None]
    dst = jnp.arange(gb * w)[None, None, :]
    place_tc = (dst == (src // S5_CH) * (gb * S5_CH) + grp * S5_CH + src % S5_CH).astype(F32)
    src = jnp.arange(4 * p)[None, :, None]
    dst = jnp.arange(4 * gb * p)[None, None, :]
    place_st = (dst == (src // p) * (gb * p) + grp * p + src % p).astype(F32)
    pack = lambda left, mat, right: jnp.einsum('gri,ogrs,gsj->oij', left, mat.reshape((nb, gb) + mat.shape[1:]),
                                               right).astype(BF16)
    lam_blk = lam.reshape(nb, gb, 2, 2, p).transpose(0, 2, 3, 1, 4).reshape(nb, 2, 2 * gb * p)
    return (pack(place_tc, m_mat, place_tc), pack(place_tc, e_mat, place_st), pack(place_st, r_mat, place_tc),
            lam_blk)


def _s5_kernel(u_ref, m_ref, e_ref, r_ref, lam_ref, y_ref, d_scr, sp_scr, *, n_ctx_chunks, n_chunks):
    u = u_ref[...]
    d_scr[...] = jnp.dot(u, e_ref[...], preferred_element_type=F32)
    sl = S5_GB * S5_STATE
    lam_re = jnp.broadcast_to(lam_ref[0:1, :], (SUBLANES, 2 * sl))
    lam_im = jnp.broadcast_to(lam_ref[1:2, :], (SUBLANES, 2 * sl))

    def rows(n):
        return pl.ds(pl.multiple_of(n * SUBLANES, SUBLANES), SUBLANES)

    def step(s, carry):
        s_re, s_im = carry
        nf = s
        nb = jnp.where(s < n_ctx_chunks, n_ctx_chunks - 1 - s, n_chunks - 1 - (s - n_ctx_chunks))
        sp_scr[rows(nf), 0:sl] = s_re[:, 0:sl]
        sp_scr[rows(nf), 2 * sl:3 * sl] = s_im[:, 0:sl]
        sp_scr[rows(nb), sl:2 * sl] = s_re[:, sl:2 * sl]
        sp_scr[rows(nb), 3 * sl:4 * sl] = s_im[:, sl:2 * sl]
        d_re = jnp.concatenate([d_scr[rows(nf), 0:sl], d_scr[rows(nb), sl:2 * sl]], axis=1)
        d_im = jnp.concatenate([d_scr[rows(nf), 2 * sl:3 * sl], d_scr[rows(nb), 3 * sl:4 * sl]], axis=1)
        n_re = lam_re * s_re - lam_im * s_im + d_re
        n_im = lam_re * s_im + lam_im * s_re + d_im
        return n_re, n_im

    zero = jnp.zeros((SUBLANES, 2 * sl), F32)
    lax.fori_loop(0, n_chunks, step, (zero, zero))
    y_ref[...] = (jnp.dot(u, m_ref[...], preferred_element_type=F32)
                  + jnp.dot(sp_scr[...].astype(BF16), r_ref[...], preferred_element_type=F32)).astype(y_ref.dtype)


def s5_scan(u_g, m_mat, e_mat, r_mat, lam, n_ctx_chunks, n_chunks):
    nb, r, w = u_g.shape
    sw = e_mat.shape[2]
    return pl.pallas_call(
        functools.partial(_s5_kernel, n_ctx_chunks=n_ctx_chunks, n_chunks=n_chunks),
        grid=(nb,),
        in_specs=[pl.BlockSpec((None, r, w), lambda i: (i, 0, 0)),
                  pl.BlockSpec((None, w, w), lambda i: (i, 0, 0)),
                  pl.BlockSpec((None, w, sw), lambda i: (i, 0, 0)),
                  pl.BlockSpec((None, sw, w), lambda i: (i, 0, 0)),
                  pl.BlockSpec((None, 2, sw // 2), lambda i: (i, 0, 0))],
        out_specs=pl.BlockSpec((None, r, w), lambda i: (i, 0, 0)),
        out_shape=jax.ShapeDtypeStruct((nb, r, w), BF16),
        scratch_shapes=[pltpu.VMEM((r, sw), F32), pltpu.VMEM((r, sw), F32)],
        compiler_params=_cparams("arbitrary"),
        name="s5_scan",
    )(u_g, m_mat, e_mat, r_mat, lam)


def s5_layer(u_all, bsz, seq, n_ctx, params):
    m_mat, e_mat, r_mat, lam = s5_group_matrices(*params)
    width = u_all.shape[1]
    bw = S5_GB * S5_CH
    nb = width // bw
    t = S5_T

    def to_groups(u, length):
        u = u.reshape(bsz, length // t, t, nb, bw).transpose(3, 1, 0, 2, 4)
        u = jnp.pad(u, ((0, 0), (0, 0), (0, SUBLANES - bsz), (0, 0), (0, 0)))
        return u.reshape(nb, length // t, SUBLANES, t * bw)

    n_lat = bsz * seq
    ncc, ncl = n_ctx // t, seq // t
    u_g = jnp.concatenate([to_groups(u_all[n_lat:], n_ctx), to_groups(u_all[:n_lat], seq)], axis=1)
    u_g = u_g.reshape(nb, (ncc + ncl) * SUBLANES, t * bw).astype(BF16)
    y_g = s5_scan(u_g, m_mat, e_mat, r_mat, lam, ncc, ncc + ncl)
    y_g = y_g.reshape(nb, ncc + ncl, SUBLANES, t, bw)[:, :, :bsz]

    def from_groups(y, length):
        return y.transpose(2, 1, 3, 0, 4).reshape(bsz * length, width)

    return jnp.concatenate([from_groups(y_g[:, ncc:], seq), from_groups(y_g[:, :ncc], n_ctx)], axis=0)


def _glu_kernel(y_ref, w_ref, b_ref, o_ref):
    gl = _gelu(y_ref[...].astype(F32))
    z = jnp.dot(gl.astype(BF16), w_ref[...], preferred_element_type=F32) + b_ref[...]
    o_ref[...] = (gl * jax.nn.sigmoid(z)).astype(o_ref.dtype)


def s5_glu(y, w, b, tm=512):
    m, n = y.shape
    return pl.pallas_call(
        _glu_kernel,
        grid=(m // tm,),
        in_specs=[pl.BlockSpec((tm, n), lambda i: (i, 0)),
                  pl.BlockSpec((n, n), lambda i: (0, 0)),
                  pl.BlockSpec((1, n), lambda i: (0, 0))],
        out_specs=pl.BlockSpec((tm, n), lambda i: (i, 0)),
        out_shape=jax.ShapeDtypeStruct((m, n), BF16),
        compiler_params=_cparams("arbitrary"),
        name="s5_glu",
    )(y, w, b.reshape(1, n))


def _top_rows(s, n_take, break_ties):
    r = s.shape[0]
    rows = lax.broadcasted_iota(jnp.int32, s.shape, 0).astype(F32) if break_ties else None
    rank = jnp.full(s.shape, NOT_SELECTED, F32)
    vals = []
    for k in range(n_take):
        m = jnp.max(s, axis=0, keepdims=True)
        if break_ties:
            idx = jnp.min(jnp.where(s == m, rows, float(r)), axis=0, keepdims=True)
            sel = rows == idx
        else:
            sel = s == m
        rank = jnp.where(sel, float(k), rank)
        s = jnp.where(sel, -jnp.inf, s)
        vals.append(m)
    return vals, rank, s


_CAND_WIDTHS = [PEER_TOPK // (i + 1) for i in range(PEER_TOPK)]
_CAND_STARTS = [sum(_CAND_WIDTHS[:i]) for i in range(PEER_TOPK + 1)]
_CAND_ROWS = -(-_CAND_STARTS[-1] // SUBLANES) * SUBLANES


def _route_head(s1, s2, grp, break_ties):
    v1, rank1, _ = _top_rows(s1, PEER_TOPK, break_ties)
    v2, rank2, _ = _top_rows(s2, PEER_TOPK, break_ties)
    v2m = jnp.concatenate(v2, axis=0)
    pad = jnp.full((_CAND_ROWS - _CAND_STARTS[-1], v2m.shape[1]), -jnp.inf, F32)
    cand = jnp.concatenate([v1[i] + v2m[0:_CAND_WIDTHS[i]] for i in range(PEER_TOPK)] + [pad], axis=0)
    cv, _, left = _top_rows(cand, PEER_TOPK, break_ties)
    picked = jnp.where(left != cand, 1.0, 0.0)
    picked = jnp.concatenate([picked, jnp.zeros((grp.shape[1] - _CAND_ROWS, picked.shape[1]), F32)], axis=0)
    cnt = jnp.dot(grp, picked, preferred_element_type=F32)
    z = jnp.zeros_like(cv[0])
    for k in range(PEER_TOPK):
        z = z + jnp.exp(cv[k] - cv[0])
    thr = jnp.zeros_like(s1)
    for i in range(PEER_TOPK):
        thr = jnp.where(rank1 == float(i), cnt[i:i + 1, :], thr)
    n_out = (jnp.sum(jnp.where(rank1 != NOT_SELECTED, 1.0, 0.0), axis=0, keepdims=True)
             + jnp.sum(jnp.where(rank2 != NOT_SELECTED, 1.0, 0.0), axis=0, keepdims=True)
             + jnp.sum(cnt, axis=0, keepdims=True))
    return thr, jnp.exp(s1 - v1[0]), rank2.astype(BF16), (jnp.exp(s2 - v2[0]) / z).astype(BF16), n_out


def _peer_route_kernel(q_ref, k1_ref, k2_ref, grp_ref, thr_ref, a_ref, r2_ref, b_ref):
    hp = lax.Precision.HIGHEST
    nt = (((1,), (1,)), ((), ()))
    for h in range(PEER_HEADS):
        qh = q_ref[:, h * 2 * PEER_HALF:(h + 1) * 2 * PEER_HALF]
        s1 = lax.dot_general(k1_ref[h], qh, nt, precision=hp, preferred_element_type=F32)
        s2 = lax.dot_general(k2_ref[h], qh, nt, precision=hp, preferred_element_type=F32)

        def write(res, h=h):
            thr_ref[h], a_ref[h], r2_ref[h], b_ref[h] = res[:4]

        fast = _route_head(s1, s2, grp_ref[...], break_ties=False)
        write(fast)

        @pl.when(jnp.max(fast[4]) > 3.0 * PEER_TOPK)
        def _(s1=s1, s2=s2, write=write):
            write(_route_head(s1, s2, grp_ref[...], break_ties=True))


def peer_route(q, keys, tt=256):
    t = q.shape[0]
    zeros = jnp.zeros((PEER_HEADS, PEER_NKEYS, PEER_HALF), F32)
    k1 = jnp.concatenate([keys[:, 0], zeros], axis=-1)
    k2 = jnp.concatenate([zeros, keys[:, 1]], axis=-1)
    row = jnp.arange(LANES)[None, :]
    grp = ((row >= jnp.array(_CAND_STARTS[:-1])[:, None]) & (row < jnp.array(_CAND_STARTS[1:])[:, None])).astype(F32)
    shp32 = jax.ShapeDtypeStruct((PEER_HEADS, PEER_NKEYS, t), F32)
    shp16 = jax.ShapeDtypeStruct((PEER_HEADS, PEER_NKEYS, t), BF16)
    ospec = pl.BlockSpec((PEER_HEADS, PEER_NKEYS, tt), lambda i: (0, 0, i))
    kspec = pl.BlockSpec((PEER_HEADS, PEER_NKEYS, 2 * PEER_HALF), lambda i: (0, 0, 0))
    return pl.pallas_call(
        _peer_route_kernel,
        grid=(t // tt,),
        in_specs=[pl.BlockSpec((tt, PEER_HEADS * 2 * PEER_HALF), lambda i: (i, 0)), kspec, kspec,
                  pl.BlockSpec((PEER_TOPK, LANES), lambda i: (0, 0))],
        out_specs=[ospec] * 4,
        out_shape=[shp32, shp32, shp16, shp16],
        compiler_params=_cparams("arbitrary"),
        name="peer_route",
    )(q, k1, k2, grp)


def _rows_bf16(row, n_rows):
    packed = jnp.broadcast_to(row, (2 * SUBLANES, row.shape[1])).astype(BF16)
    return jnp.concatenate([packed] * (n_rows // (2 * SUBLANES)), axis=0)


def _peer_dense_kernel(ft_ref, u_ref, vt_ref, thr_ref, a_ref, r2_ref, b_ref, gate_ref, res_ref, ng_ref, o_ref,
                       acc_scr, ge_scr, *, rows_per_tile, final_norm):
    j = pl.program_id(1)

    @pl.when(j == 0)
    def _():
        acc_scr[...] = jnp.zeros_like(acc_scr)
        ge_scr[...] = jnp.zeros_like(ge_scr)

    w_rows = []
    for r in range(rows_per_tile):
        w = None
        for h in range(PEER_HEADS):
            thr = _rows_bf16(thr_ref[h, r:r + 1, :], PEER_NKEYS)
            a = _rows_bf16(a_ref[h, r:r + 1, :], PEER_NKEYS)
            b = b_ref[h]
            term = jnp.where(r2_ref[h] < thr, b, jnp.zeros_like(b)) * a
            w = term if w is None else w + term
        w_rows.append(w * ge_scr[r * PEER_NKEYS:(r + 1) * PEER_NKEYS, :])
    wt = jnp.concatenate(w_rows, axis=0)
    acc_scr[...] += jnp.dot(vt_ref[...], wt, preferred_element_type=F32)
    act = jnp.dot(u_ref[...], ft_ref[...], preferred_element_type=F32)
    ge_scr[...] = _gelu(act).astype(BF16)

    @pl.when(j == pl.num_programs(1) - 1)
    def _():
        hn = res_ref[...] + gate_ref[...] * acc_scr[...].T
        if final_norm:
            ms = jnp.mean(hn * hn, axis=-1, keepdims=True)
            hn = hn * lax.rsqrt(ms + EPS) * ng_ref[...]
        o_ref[...] = hn


def peer_dense(ft, u, vt, layer, thr, a, r2, b, modr, gate_idx, res, mod_row, norm_g=None, tt=512,
               rows_per_tile=8):
    d, t = ft.shape
    e = u.shape[1]
    te = rows_per_tile * PEER_NKEYS
    ne = e // te

    def cur(j):
        return jnp.minimum(j, ne - 1)

    def prev(j):
        return jnp.maximum(j - 1, 0)

    rspec = pl.BlockSpec((PEER_HEADS, rows_per_tile, tt), lambda i, j: (0, prev(j), i))
    cspec = pl.BlockSpec((PEER_HEADS, PEER_NKEYS, tt), lambda i, j: (0, 0, i))
    return pl.pallas_call(
        functools.partial(_peer_dense_kernel, rows_per_tile=rows_per_tile, final_norm=norm_g is not None),
        grid=(t // tt, ne + 1),
        in_specs=[pl.BlockSpec((d, tt), lambda i, j: (0, i)),
                  pl.BlockSpec((None, te, d), lambda i, j: (layer, cur(j), 0)),
                  pl.BlockSpec((None, d, te), lambda i, j: (layer, 0, prev(j))),
                  rspec, rspec, cspec, cspec,
                  pl.BlockSpec((None, 1, d), lambda i, j: (mod_row(i * tt) * N_MOD + gate_idx, 0, 0)),
                  pl.BlockSpec((tt, d), lambda i, j: (i, 0)),
                  pl.BlockSpec((1, d), lambda i, j: (0, 0))],
        out_specs=pl.BlockSpec((tt, d), lambda i, j: (i, 0)),
        out_shape=jax.ShapeDtypeStruct((t, d), F32),
        scratch_shapes=[pltpu.VMEM((d, tt), F32), pltpu.VMEM((te, tt), BF16)],
        compiler_params=_cparams("arbitrary", "arbitrary"),
        name="peer_dense",
    )(ft, u, vt, thr, a, r2, b, modr, res, (jnp.ones((d,), F32) if norm_g is None else norm_g).reshape(1, d))


def peer_ffn(h, modr, mod_row, norm_g, wq, keys, u, vt, layer, tm, final_norm_g=None):
    q, ft = norm_mod_matmul(h, norm_g, modr, 3, 4, wq.astype(BF16), mod_row, tm=tm, tn=512, emit_a=True)
    thr, a, r2, b = peer_route(q, keys)
    return peer_dense(ft, u, vt, layer, thr, a, r2, b, modr, 5, h, mod_row, norm_g=final_norm_g)


def _conv_kernel(x_ref, w_ref, b_ref, o_ref, pad_scr, *, length, row_tile):
    half = SSD_CONV // 2
    zeros = jnp.zeros((SUBLANES, x_ref.shape[1]), F32)
    pad_scr[0:SUBLANES, :] = zeros
    pad_scr[SUBLANES + length:2 * SUBLANES + length, :] = zeros
    pad_scr[SUBLANES:SUBLANES + length, :] = x_ref[...].astype(F32)
    for r0 in range(0, length, row_tile):
        acc = jnp.broadcast_to(b_ref[...], (row_tile, x_ref.shape[1]))
        for k in range(SSD_CONV):
            start = SUBLANES + r0 + k - half
            acc = acc + w_ref[k:k + 1, :] * pad_scr[start:start + row_tile, :]
        o_ref[r0:r0 + row_tile, :] = jax.nn.silu(acc).astype(o_ref.dtype)


def ssd_conv(p, col0, width, row0, n_seq, length, conv_w, conv_b, tc=256):
    row_tile = min(length, 512)
    return pl.pallas_call(
        functools.partial(_conv_kernel, length=length, row_tile=row_tile),
        grid=(n_seq, width // tc),
        in_specs=[pl.BlockSpec((length, tc), lambda s, j: (row0 // length + s, col0 // tc + j)),
                  pl.BlockSpec((SSD_CONV, tc), lambda s, j: (0, j)),
                  pl.BlockSpec((1, tc), lambda s, j: (0, j))],
        out_specs=pl.BlockSpec((length, tc), lambda s, j: (s, j)),
        out_shape=jax.ShapeDtypeStruct((n_seq * length, width), BF16),
        scratch_shapes=[pltpu.VMEM((length + 2 * SUBLANES, tc), F32)],
        compiler_params=_cparams("arbitrary", "arbitrary"),
        name="ssd_conv",
    )(p, conv_w, conv_b.reshape(1, width))


def _ssd_kernel(xl_ref, bl_ref, cl_ref, xx_ref, bx_ref, cx_ref, dt_ref, bias_ref, alog_ref, y_ref, st_scr, *,
                reverse, n_ctx_chunks):
    hp = lax.Precision.HIGHEST
    ch = SSD_CHUNK
    nh = dt_ref.shape[1]
    hpg = nh // SSD_GROUPS
    is_ctx = pl.program_id(1) < n_ctx_chunks

    def pick(ctx_ref, lat_ref, cols):
        return jnp.where(is_ctx, ctx_ref[:, cols], lat_ref[:, cols])

    @pl.when(pl.program_id(1) == 0)
    def _():
        st_scr[...] = jnp.zeros_like(st_scr)

    dt = jax.nn.softplus(dt_ref[...] + bias_ref[...])
    la = dt * (-jnp.exp(alog_ref[...]))
    row = lax.broadcasted_iota(jnp.int32, (ch, ch), 0)
    col = lax.broadcasted_iota(jnp.int32, (ch, ch), 1)
    keep = (col >= row) if reverse else (row >= col)
    tri = jnp.where(keep, 1.0, 0.0)
    cs = jnp.dot(tri, la, precision=hp, preferred_element_type=F32)
    both_t = jnp.concatenate([cs, dt], axis=1).T
    cs_t, dt_t = both_t[0:nh], both_t[nh:2 * nh]
    end = 0 if reverse else ch - 1
    tot_t = cs_t[:, end:end + 1]
    ws_t = jnp.exp(tot_t - cs_t) * dt_t
    dec_t = jnp.exp(tot_t)
    ecs = jnp.exp(cs)
    lo = lax.broadcasted_iota(jnp.int32, (ch, 2 * SSD_HEAD_DIM), 1) < SSD_HEAD_DIM
    nt = (((1,), (1,)), ((), ()))
    for g in range(SSD_GROUPS):
        gs = slice(g * SSD_STATE, (g + 1) * SSD_STATE)
        cg = pick(cx_ref, cl_ref, gs)
        bg = pick(bx_ref, bl_ref, gs)
        cb = lax.dot_general(cg, bg, nt, preferred_element_type=F32)
        bg_t = bg.astype(F32).T
        cg32 = cg.astype(F32)
        for pr in range(hpg // 2):
            lanes = slice((g * hpg + 2 * pr) * SSD_HEAD_DIM, (g * hpg + 2 * pr + 2) * SSD_HEAD_DIM)
            sl = slice(2 * pr * SSD_HEAD_DIM, (2 * pr + 2) * SSD_HEAD_DIM)
            xp = pick(xx_ref, xl_ref, lanes)
            st = st_scr[g, :, sl]
            rhs = jnp.concatenate([xp, st.astype(BF16)], axis=0)
            ys, news, decs = [], [], []
            for j in range(2):
                h = g * hpg + 2 * pr + j
                seg = cs[:, h:h + 1] - cs_t[h:h + 1, :]
                gmat = jnp.where(keep, jnp.exp(seg), 0.0) * cb * dt_t[h:h + 1, :]
                cmat = cg32 * ecs[:, h:h + 1]
                lhs = jnp.concatenate([gmat, cmat], axis=1).astype(BF16)
                ys.append(jnp.dot(lhs, rhs, preferred_element_type=F32))
                bw = (bg_t * ws_t[h:h + 1, :]).astype(BF16)
                news.append(jnp.dot(bw, xp, preferred_element_type=F32))
                decs.append(jnp.broadcast_to(dec_t[h:h + 1, :], (ch, 2 * SSD_HEAD_DIM)))
            y_ref[:, lanes] = jnp.where(lo, ys[0], ys[1]).astype(y_ref.dtype)
            st_scr[g, :, sl] = st * jnp.where(lo, decs[0], decs[1]) + jnp.where(lo, news[0], news[1])


def ssd_scan(xc_lat, xc_ctx, dt_raw, dt_bias, a_log, bsz, seq, n_ctx, reverse):
    nh = dt_raw.shape[1]
    inner = nh * SSD_HEAD_DIM
    gn = SSD_GROUPS * SSD_STATE
    nl, nc = seq // SSD_CHUNK, n_ctx // SSD_CHUNK
    lat_blocks = bsz * nl

    def lblk(b, s):
        sl = jnp.maximum(s - nc, 0)
        return b * nl + ((nl - 1 - sl) if reverse else sl)

    def cblk(b, s):
        sc = jnp.minimum(s, nc - 1)
        return b * nc + ((nc - 1 - sc) if reverse else sc)

    def blk(b, s):
        return jnp.where(s < nc, lat_blocks + cblk(b, s), lblk(b, s))

    def xbc_specs(fn):
        return [pl.BlockSpec((SSD_CHUNK, inner), lambda b, s: (fn(b, s), 0)),
                pl.BlockSpec((SSD_CHUNK, gn), lambda b, s: (fn(b, s), inner // gn)),
                pl.BlockSpec((SSD_CHUNK, gn), lambda b, s: (fn(b, s), inner // gn + 1))]

    return pl.pallas_call(
        functools.partial(_ssd_kernel, reverse=reverse, n_ctx_chunks=nc),
        grid=(bsz, nc + nl),
        in_specs=xbc_specs(lblk) + xbc_specs(cblk) + [
            pl.BlockSpec((SSD_CHUNK, nh), lambda b, s: (blk(b, s), 0)),
            pl.BlockSpec((1, nh), lambda b, s: (0, 0)),
            pl.BlockSpec((1, nh), lambda b, s: (0, 0))],
        out_specs=pl.BlockSpec((SSD_CHUNK, inner), lambda b, s: (lblk(b, s), 0)),
        out_shape=jax.ShapeDtypeStruct((bsz * seq, inner), BF16),
        scratch_shapes=[pltpu.VMEM((SSD_GROUPS, SSD_STATE, inner // SSD_GROUPS), F32)],
        compiler_params=_cparams("arbitrary", "arbitrary"),
        name="ssd_scan_bwd" if reverse else "ssd_scan_fwd",
    )(xc_lat, xc_lat, xc_lat, xc_ctx, xc_ctx, xc_ctx, dt_raw, dt_bias.reshape(1, nh), a_log.reshape(1, nh))


def _ssd_out_kernel(yf_ref, yb_ref, x_ref, z_ref, d_ref, ng_ref, w_ref, gate_ref, res_ref, o_ref, a_scr):
    @pl.when(pl.program_id(1) == 0)
    def _():
        y = (x_ref[...].astype(F32) * d_ref[...] + yf_ref[...].astype(F32) + yb_ref[...].astype(F32))
        gv = y * jax.nn.silu(z_ref[...].astype(F32))
        gw = gv.shape[1] // SSD_GROUPS
        for g in range(SSD_GROUPS):
            part = gv[:, g * gw:(g + 1) * gw]
            ms = jnp.mean(part * part, axis=-1, keepdims=True)
            a_scr[:, g * gw:(g + 1) * gw] = (part * lax.rsqrt(ms + EPS) * ng_ref[:, g * gw:(g + 1) * gw]).astype(BF16)

    acc = jnp.dot(a_scr[...], w_ref[...], preferred_element_type=F32)
    o_ref[...] = res_ref[...] + gate_ref[...] * acc


def ssd_output(yf, yb, xc, pz, d_exp, norm_g, w_out, modr, res, mod_row, tm=512, tn=512):
    m, inner = yf.shape
    n = w_out.shape[1]
    return pl.pallas_call(
        _ssd_out_kernel,
        grid=(m // tm, n // tn),
        in_specs=[pl.BlockSpec((tm, inner), lambda i, j: (i, 0)),
                  pl.BlockSpec((tm, inner), lambda i, j: (i, 0)),
                  pl.BlockSpec((tm, inner), lambda i, j: (i, 0)),
                  pl.BlockSpec((tm, inner), lambda i, j: (i, 0)),
                  pl.BlockSpec((1, inner), lambda i, j: (0, 0)),
                  pl.BlockSpec((1, inner), lambda i, j: (0, 0)),
                  pl.BlockSpec((inner, tn), lambda i, j: (0, j)),
                  pl.BlockSpec((None, 1, tn), lambda i, j: (mod_row(i * tm) * N_MOD + 2, 0, j)),
                  pl.BlockSpec((tm, tn), lambda i, j: (i, j))],
        out_specs=pl.BlockSpec((tm, tn), lambda i, j: (i, j)),
        out_shape=jax.ShapeDtypeStruct((m, n), F32),
        scratch_shapes=[pltpu.VMEM((tm, inner), BF16)],
        compiler_params=_cparams("arbitrary", "arbitrary"),
        name="ssd_output",
    )(yf, yb, xc, pz, d_exp.reshape(1, inner), norm_g.reshape(1, inner), w_out, modr, res)


def _mod_rows(c, c_ctx, w, b, layer):
    bsz, d = c.shape
    cond = jnp.zeros((SUBLANES, d), F32).at[:bsz].set(c).at[bsz].set(c_ctx)
    mod = modulation(cond, w, b, layer)
    return mod.reshape(SUBLANES * N_MOD, 1, d)


def attn_s5_layer(h, bsz, seq, n_ctx, modr, mod_row, tm, norm1_g, w_in, sink, s5_params, glu_w, glu_b, w_out):
    n_lat = bsz * seq
    qw = ATTN_HEADS * HEAD_DIM
    kvw = ATTN_KV_HEADS * HEAD_DIM
    p = norm_mod_matmul(h, norm1_g, modr, 0, 1, w_in.astype(BF16), mod_row, tm=tm, tn=512, out_dtype=BF16)
    qk = rope_qk(p, n_lat, seq, qw + kvw)
    attn_lat = attention_latent(qk, p, sink, bsz, seq, n_ctx)
    attn_ctx = attention_context(p, sink, bsz, seq, n_ctx)
    attn = jnp.concatenate([attn_lat, attn_ctx], axis=0)
    y = s5_layer(p[:, qw + 2 * kvw:], bsz, seq, n_ctx, s5_params)
    s5o = s5_glu(y, glu_w.astype(BF16), glu_b)
    w_out = w_out.astype(BF16)
    return matmul_gated_residual([attn, s5o], [w_out[:qw], w_out[qw:]], modr, 2, h, mod_row, tm=tm, tn=512)


def ssd_layer(h, bsz, seq, n_ctx, modr, mod_row, tm, norm1_g, w_in, conv_w, conv_b, dt_bias, a_log, d_skip,
              norm_g, w_out):
    n_lat = bsz * seq
    nh = a_log.shape[1]
    inner = nh * SSD_HEAD_DIM
    xbc_w = inner + 2 * SSD_GROUPS * SSD_STATE
    w_in = w_in.astype(BF16)
    pz = norm_mod_matmul(h, norm1_g, modr, 0, 1, w_in[:, :inner + xbc_w], mod_row, tm=tm, tn=1024,
                         out_dtype=BF16)
    dt_raw = norm_mod_matmul(h, norm1_g, modr, 0, 1, w_in[:, inner + xbc_w:], mod_row, tm=tm, tn=2 * nh)
    xc = ssd_conv(pz, inner, xbc_w, 0, bsz, seq, conv_w, conv_b)
    xc_ctx = ssd_conv(pz, inner, xbc_w, n_lat, bsz, n_ctx, conv_w, conv_b)
    yf = ssd_scan(xc, xc_ctx, dt_raw[:, :nh], dt_bias[0], a_log[0], bsz, seq, n_ctx, reverse=False)
    yb = ssd_scan(xc, xc_ctx, dt_raw[:, nh:], dt_bias[1], a_log[1], bsz, seq, n_ctx, reverse=True)
    d_exp = jnp.repeat(d_skip, SSD_HEAD_DIM)
    return ssd_output(yf, yb, xc, pz, d_exp, norm_g, w_out.astype(BF16), modr, h, mod_row)


def kernel(x, c, ctx, c_ctx, mod_w, mod_b, norm1_g, norm2_g, ab_w_in, attn_sink, s5_a_re, s5_a_im, s5_log_dt, s5_b_re, s5_b_im, s5_c_re, s5_c_im, s5_d, s5_glu_w, s5_glu_b, ab_w_out, ssd_w_in, ssd_conv_w, ssd_conv_b, ssd_dt_bias, ssd_a_log, ssd_d, ssd_norm_g, ssd_w_out, peer_wq, peer_keys, peer_u, peer_v, final_norm_g):
    bsz, seq, d = x.shape
    n_ctx = ctx.shape[1]
    n_lat = bsz * seq
    depth = mod_w.shape[0]
    tm = 512

    def mod_row(r):
        return jnp.minimum(r // seq, bsz)

    h = jnp.concatenate([x.reshape(n_lat, d), ctx.reshape(bsz * n_ctx, d)], axis=0)
    peer_u16 = peer_u.astype(BF16)
    peer_vt16 = peer_v.transpose(0, 2, 1).astype(BF16)
    for i in range(depth):
        last = i == depth - 1
        j = i // 2
        modr = _mod_rows(c, c_ctx, mod_w, mod_b, i)
        if i % 2 == 0:
            s5_params = (s5_a_re[j], s5_a_im[j], s5_log_dt[j], s5_b_re[j], s5_b_im[j], s5_c_re[j], s5_c_im[j],
                         s5_d[j])
            h = attn_s5_layer(h, bsz, seq, n_ctx, modr, mod_row, tm, norm1_g[i], ab_w_in[j], attn_sink[j],
                              s5_params, s5_glu_w[j], s5_glu_b[j], ab_w_out[j])
        else:
            if not last:
                raise NotImplementedError("an SSD layer that must also update the context stream")
            h = ssd_layer(h, bsz, seq, n_ctx, modr, mod_row, tm, norm1_g[i], ssd_w_in[j], ssd_conv_w[j],
                          ssd_conv_b[j], ssd_dt_bias[j], ssd_a_log[j], ssd_d[j], ssd_norm_g[j], ssd_w_out[j])
        h = peer_ffn(h, modr, mod_row, norm2_g[i], peer_wq[i], peer_keys[i], peer_u16, peer_vt16, i, tm,
                     final_norm_g=final_norm_g if last else None)
    return h[:n_lat].reshape(bsz, seq, d)
```

```python
import functools
import math

import jax
import jax.numpy as jnp
from jax import lax
from jax.experimental import pallas as pl
from jax.experimental.pallas import tpu as pltpu

F32 = jnp.float32
BF16 = jnp.bfloat16
EPS = 1e-6

V7X_VMEM_LIMIT_BYTES = 56 * 1024 * 1024
LANES = 128
SUBLANES = 8

N_MOD = 6
HEAD_DIM = 128
ATTN_HEADS = 8
ATTN_KV_HEADS = 2
ATTN_GROUP = ATTN_HEADS // ATTN_KV_HEADS
ATTN_SCALE = HEAD_DIM ** -0.5
BLOCK = 128
GRID_W = 64
ROPE_THETA = 10000.0
S5_CH = 16
S5_STATE = 64
S5_T = 16
S5_GB = 2
SSD_HEAD_DIM = 64
SSD_GROUPS = 8
SSD_STATE = 128
SSD_CONV = 5
SSD_CHUNK = 128
PEER_HEADS = 8
PEER_NKEYS = 128
PEER_HALF = 64
PEER_TOPK = 16
NOT_SELECTED = 127.0


def _cparams(*sem):
    return pltpu.CompilerParams(dimension_semantics=sem, vmem_limit_bytes=V7X_VMEM_LIMIT_BYTES)


def _gelu(x):
    return 0.5 * x * (1.0 + lax.erf(x * (2.0 ** -0.5)))


def _mod_kernel(c_ref, w_ref, b_ref, o_ref):
    s = jax.nn.silu(c_ref[...]).astype(BF16)
    o_ref[...] = jnp.dot(s, w_ref[...].astype(BF16), preferred_element_type=F32) + b_ref[...]


def modulation(cond, w, b, layer, tn=512):
    rows, d = cond.shape
    n = w.shape[2]
    return pl.pallas_call(
        _mod_kernel,
        grid=(n // tn,),
        in_specs=[pl.BlockSpec((rows, d), lambda j: (0, 0)),
                  pl.BlockSpec((None, d, tn), lambda j: (layer, 0, j)),
                  pl.BlockSpec((None, 1, tn), lambda j: (layer, 0, j))],
        out_specs=pl.BlockSpec((rows, tn), lambda j: (0, j)),
        out_shape=jax.ShapeDtypeStruct((rows, n), F32),
        compiler_params=_cparams("arbitrary"),
        name="modulation",
    )(cond, w, b.reshape(b.shape[0], 1, n))


def _nmm_kernel(x_ref, g_ref, sh_ref, sc_ref, w_ref, *refs, emit_a):
    if emit_a:
        o_ref, a_out_ref, a_scr = refs
    else:
        o_ref, a_scr = refs

    @pl.when(pl.program_id(1) == 0)
    def _():
        x = x_ref[...]
        ms = jnp.mean(x * x, axis=-1, keepdims=True)
        y = x * lax.rsqrt(ms + EPS) * g_ref[...]
        a = y * (1.0 + sc_ref[...]) + sh_ref[...]
        a_scr[...] = a.astype(BF16)
        if emit_a:
            a_out_ref[...] = a.T.astype(BF16)

    o_ref[...] = jnp.dot(a_scr[...], w_ref[...], preferred_element_type=F32).astype(o_ref.dtype)


def norm_mod_matmul(x, g, modr, shift_idx, scale_idx, w, mod_row, tm, tn, out_dtype=F32, emit_a=False):
    m, d = x.shape
    n = w.shape[1]
    out_shape = [jax.ShapeDtypeStruct((m, n), out_dtype)]
    out_specs = [pl.BlockSpec((tm, tn), lambda i, j: (i, j))]
    if emit_a:
        out_shape.append(jax.ShapeDtypeStruct((d, m), BF16))
        out_specs.append(pl.BlockSpec((d, tm), lambda i, j: (0, i)))
    res = pl.pallas_call(
        functools.partial(_nmm_kernel, emit_a=emit_a),
        grid=(m // tm, n // tn),
        in_specs=[pl.BlockSpec((tm, d), lambda i, j: (i, 0)),
                  pl.BlockSpec((1, d), lambda i, j: (0, 0)),
                  pl.BlockSpec((None, 1, d), lambda i, j: (mod_row(i * tm) * N_MOD + shift_idx, 0, 0)),
                  pl.BlockSpec((None, 1, d), lambda i, j: (mod_row(i * tm) * N_MOD + scale_idx, 0, 0)),
                  pl.BlockSpec((d, tn), lambda i, j: (0, j))],
        out_specs=out_specs,
        out_shape=out_shape,
        scratch_shapes=[pltpu.VMEM((tm, d), BF16)],
        compiler_params=_cparams("arbitrary", "arbitrary"),
        name="norm_mod_matmul",
    )(x, g.reshape(1, d), modr, modr, w)
    return res if emit_a else res[0]


def _mmres_kernel(*refs, n_parts):
    xs = refs[:n_parts]
    ws = refs[n_parts:2 * n_parts]
    gate_ref, res_ref, o_ref = refs[2 * n_parts:]
    acc = jnp.dot(xs[0][...], ws[0][...], preferred_element_type=F32)
    for p in range(1, n_parts):
        acc = acc + jnp.dot(xs[p][...], ws[p][...], preferred_element_type=F32)
    o_ref[...] = res_ref[...] + gate_ref[...] * acc


def matmul_gated_residual(xs, ws, modr, gate_idx, res, mod_row, tm, tn):
    m, n = res.shape
    n_parts = len(xs)
    in_specs = ([pl.BlockSpec((tm, x.shape[1]), lambda i, j: (i, 0)) for x in xs]
                + [pl.BlockSpec((w.shape[0], tn), lambda i, j: (0, j)) for w in ws]
                + [pl.BlockSpec((None, 1, tn), lambda i, j: (mod_row(i * tm) * N_MOD + gate_idx, 0, j)),
                   pl.BlockSpec((tm, tn), lambda i, j: (i, j))])
    return pl.pallas_call(
        functools.partial(_mmres_kernel, n_parts=n_parts),
        grid=(m // tm, n // tn),
        in_specs=in_specs,
        out_specs=pl.BlockSpec((tm, tn), lambda i, j: (i, j)),
        out_shape=jax.ShapeDtypeStruct((m, n), F32),
        compiler_params=_cparams("arbitrary", "arbitrary"),
        name="matmul_gated_residual",
    )(*xs, *ws, modr, res)


def rope_tables(seq):
    t = jnp.arange(seq)
    row = (t // GRID_W).astype(F32)
    col = (t % GRID_W).astype(F32)
    n_freq = HEAD_DIM // 4
    inv = ROPE_THETA ** (-jnp.arange(n_freq, dtype=F32) / n_freq)
    ang = jnp.concatenate([row[:, None] * inv, col[:, None] * inv], axis=-1)
    cos, sin = jnp.cos(ang), jnp.sin(ang)
    return jnp.concatenate([cos, cos], axis=-1), jnp.concatenate([-sin, sin], axis=-1)


def _attn_group(q, k, v, mask, sink_col):
    s = lax.dot_general(q, k, (((1,), (1,)), ((), ())), preferred_element_type=F32) * ATTN_SCALE
    if mask is not None:
        s = jnp.where(mask, s, -jnp.inf)
    m = jnp.maximum(jnp.max(s, axis=-1, keepdims=True), sink_col)
    e = jnp.exp(s - m)
    den = jnp.sum(e, axis=-1, keepdims=True) + jnp.exp(sink_col - m)
    o = jnp.dot(e.astype(BF16), v, preferred_element_type=F32)
    return o / den


def _sink_col(sink_ref, hk):
    return jnp.concatenate([jnp.full((BLOCK, 1), sink_ref[hk * ATTN_GROUP + g], F32)
                            for g in range(ATTN_GROUP)], axis=0)


def _rope(x, cos_ref, sin_ref):
    x = x.astype(F32)
    return (x * cos_ref[...] + pltpu.roll(x, HEAD_DIM // 2, 1) * sin_ref[...]).astype(BF16)


def _attn_lat_kernel(sink_ref, q_ref, kp_ref, kc_ref, kn_ref, vp_ref, vc_ref, vn_ref, kx_ref, vx_ref,
                     cp_ref, cc_ref, cn_ref, sp_ref, sc_ref, sn_ref, o_ref, *, n_blocks, n_ctx):
    n = pl.program_id(1)
    row = lax.broadcasted_iota(jnp.int32, (BLOCK, BLOCK), 0)
    col = lax.broadcasted_iota(jnp.int32, (BLOCK, BLOCK), 1)
    m_prev = (col >= row) & (n > 0)
    m_next = (col <= row) & (n < n_blocks - 1)
    ones_ctx = jnp.ones((BLOCK, n_ctx), jnp.bool_)
    ones_cur = jnp.ones((BLOCK, BLOCK), jnp.bool_)
    mask1 = jnp.concatenate([ones_ctx, m_prev, ones_cur, m_next], axis=1)
    mask = jnp.concatenate([mask1] * ATTN_GROUP, axis=0)
    for hk in range(ATTN_KV_HEADS):
        hs = slice(hk * HEAD_DIM, (hk + 1) * HEAD_DIM)
        k = jnp.concatenate([kx_ref[:, hs].astype(BF16), _rope(kp_ref[:, hs], cp_ref, sp_ref),
                             _rope(kc_ref[:, hs], cc_ref, sc_ref), _rope(kn_ref[:, hs], cn_ref, sn_ref)], axis=0)
        v = jnp.concatenate([vx_ref[:, hs], vp_ref[:, hs], vc_ref[:, hs], vn_ref[:, hs]], axis=0).astype(BF16)
        q = jnp.concatenate([_rope(q_ref[:, (hk * ATTN_GROUP + g) * HEAD_DIM:(hk * ATTN_GROUP + g + 1) * HEAD_DIM],
                                   cc_ref, sc_ref) for g in range(ATTN_GROUP)], axis=0)
        o = _attn_group(q, k, v, mask, _sink_col(sink_ref, hk))
        for g in range(ATTN_GROUP):
            h = hk * ATTN_GROUP + g
            o_ref[:, h * HEAD_DIM:(h + 1) * HEAD_DIM] = o[g * BLOCK:(g + 1) * BLOCK].astype(o_ref.dtype)


def attention_latent(p, sink, bsz, seq, n_ctx):
    cos, sin = rope_tables(seq)
    nb = seq // BLOCK
    kvw = ATTN_KV_HEADS * HEAD_DIM
    qw = ATTN_HEADS * HEAD_DIM
    kcol, vcol = qw // kvw, (qw + kvw) // kvw
    ctx_blk0 = bsz * seq // n_ctx

    def prev(b, n):
        return b * nb + jnp.maximum(n - 1, 0)

    def nxt(b, n):
        return b * nb + jnp.minimum(n + 1, nb - 1)

    return pl.pallas_call(
        functools.partial(_attn_lat_kernel, n_blocks=nb, n_ctx=n_ctx),
        grid=(bsz, nb),
        in_specs=[pl.BlockSpec(memory_space=pltpu.SMEM),
                  pl.BlockSpec((BLOCK, qw), lambda b, n: (b * nb + n, 0)),
                  pl.BlockSpec((BLOCK, kvw), lambda b, n: (prev(b, n), kcol)),
                  pl.BlockSpec((BLOCK, kvw), lambda b, n: (b * nb + n, kcol)),
                  pl.BlockSpec((BLOCK, kvw), lambda b, n: (nxt(b, n), kcol)),
                  pl.BlockSpec((BLOCK, kvw), lambda b, n: (prev(b, n), vcol)),
                  pl.BlockSpec((BLOCK, kvw), lambda b, n: (b * nb + n, vcol)),
                  pl.BlockSpec((BLOCK, kvw), lambda b, n: (nxt(b, n), vcol)),
                  pl.BlockSpec((n_ctx, kvw), lambda b, n: (ctx_blk0 + b, kcol)),
                  pl.BlockSpec((n_ctx, kvw), lambda b, n: (ctx_blk0 + b, vcol))]
        + [pl.BlockSpec((BLOCK, HEAD_DIM), fn) for fn in (
            lambda b, n: (jnp.maximum(n - 1, 0), 0), lambda b, n: (n, 0),
            lambda b, n: (jnp.minimum(n + 1, nb - 1), 0))] * 2,
        out_specs=pl.BlockSpec((BLOCK, qw), lambda b, n: (b * nb + n, 0)),
        out_shape=jax.ShapeDtypeStruct((bsz * seq, qw), BF16),
        compiler_params=_cparams("arbitrary", "arbitrary"),
        name="attention_latent",
    )(sink, p, p, p, p, p, p, p, p, p, cos, cos, cos, sin, sin, sin)


def _attn_ctx_kernel(sink_ref, q_ref, k_ref, v_ref, o_ref, *, n_ctx):
    for hk in range(ATTN_KV_HEADS):
        hs = slice(hk * HEAD_DIM, (hk + 1) * HEAD_DIM)
        k = k_ref[:, hs].astype(BF16)
        v = v_ref[:, hs].astype(BF16)
        for g in range(ATTN_GROUP):
            h = hk * ATTN_GROUP + g
            q = q_ref[:, h * HEAD_DIM:(h + 1) * HEAD_DIM].astype(BF16)
            sink_col = jnp.full((n_ctx, 1), sink_ref[h], F32)
            o = _attn_group(q, k, v, None, sink_col)
            o_ref[:, h * HEAD_DIM:(h + 1) * HEAD_DIM] = o.astype(o_ref.dtype)


def attention_context(p, sink, bsz, seq, n_ctx):
    kvw = ATTN_KV_HEADS * HEAD_DIM
    qw = ATTN_HEADS * HEAD_DIM
    kcol, vcol = qw // kvw, (qw + kvw) // kvw
    ctx_blk0 = bsz * seq // n_ctx
    return pl.pallas_call(
        functools.partial(_attn_ctx_kernel, n_ctx=n_ctx),
        grid=(bsz,),
        in_specs=[pl.BlockSpec(memory_space=pltpu.SMEM),
                  pl.BlockSpec((n_ctx, qw), lambda b: (ctx_blk0 + b, 0)),
                  pl.BlockSpec((n_ctx, kvw), lambda b: (ctx_blk0 + b, kcol)),
                  pl.BlockSpec((n_ctx, kvw), lambda b: (ctx_blk0 + b, vcol))],
        out_specs=pl.BlockSpec((n_ctx, qw), lambda b: (b, 0)),
        out_shape=jax.ShapeDtypeStruct((bsz * n_ctx, qw), BF16),
        compiler_params=_cparams("arbitrary"),
        name="attention_context",
    )(sink, p, p, p)


def _cmul(ar, ai, br, bi):
    return ar * br - ai * bi, ar * bi + ai * br


def s5_group_matrices(a_re, a_im, log_dt, b_re, b_im, c_re, c_im, d):
    hp = lax.Precision.HIGHEST
    t = S5_T
    dt = jnp.exp(log_dt)
    mag = jnp.exp(dt * a_re)
    lr, li = mag * jnp.cos(dt * a_im), mag * jnp.sin(dt * a_im)
    den = a_re * a_re + a_im * a_im
    f_re = ((lr - 1) * a_re + li * a_im) / den
    f_im = (li * a_re - (lr - 1) * a_im) / den
    bb_re = f_re[..., None] * b_re - f_im[..., None] * b_im
    bb_im = f_re[..., None] * b_im + f_im[..., None] * b_re
    pr, pi = [jnp.ones_like(lr)], [jnp.zeros_like(li)]
    for _ in range(t):
        nr, ni = _cmul(pr[-1], pi[-1], lr, li)
        pr.append(nr)
        pi.append(ni)
    pw_re, pw_im = jnp.stack(pr, axis=1), jnp.stack(pi, axis=1)
    cl_re = c_re[:, None] * pw_re[:, :, :, None, :] - c_im[:, None] * pw_im[:, :, :, None, :]
    cl_im = c_re[:, None] * pw_im[:, :, :, None, :] + c_im[:, None] * pw_re[:, :, :, None, :]
    kern = jnp.einsum('dkgcp,dgpe->dkgce', jnp.concatenate([cl_re, -cl_im], axis=-1),
                      jnp.concatenate([bb_re, bb_im], axis=2), precision=hp)
    g = a_re.shape[1]
    ii = jnp.arange(t)[:, None]
    jj = jnp.arange(t)[None, :]
    kf = kern[0][jnp.clip(jj - ii, 0, t)]
    kb = kern[1][jnp.clip(ii - jj, 0, t)]
    mf = (ii <= jj)[:, :, None, None, None]
    mb = (ii >= jj)[:, :, None, None, None]
    eye = (ii == jj)[:, :, None, None, None] * jnp.eye(S5_CH, dtype=F32)[None, None, None] * d[None, None, :, :, None]
    mm = jnp.where(mf, kf, 0.0) + jnp.where(mb, kb, 0.0) + eye
    m_mat = mm.transpose(2, 0, 4, 1, 3).reshape(g, t * S5_CH, t * S5_CH)
    pf = jnp.arange(t - 1, -1, -1)
    pb = jnp.arange(t)

    def drive(dirn, pidx):
        er, ei = _cmul(pw_re[dirn][pidx][:, :, :, None], pw_im[dirn][pidx][:, :, :, None],
                       bb_re[dirn][None], bb_im[dirn][None])
        er = er.transpose(1, 0, 3, 2).reshape(g, t * S5_CH, S5_STATE)
        ei = ei.transpose(1, 0, 3, 2).reshape(g, t * S5_CH, S5_STATE)
        return er, ei

    ef_re, ef_im = drive(0, pf)
    eb_re, eb_im = drive(1, pb)
    e_mat = jnp.concatenate([ef_re, eb_re, ef_im, eb_im], axis=-1)
    rf = jnp.arange(1, t + 1)
    rb = jnp.arange(t, 0, -1)

    def read(dirn, pidx):
        rr = cl_re[dirn][pidx]
        ri = cl_im[dirn][pidx]
        rr = rr.transpose(1, 3, 0, 2).reshape(g, S5_STATE, t * S5_CH)
        ri = ri.transpose(1, 3, 0, 2).reshape(g, S5_STATE, t * S5_CH)
        return rr, -ri

    rf_re, rf_im = read(0, rf)
    rb_re, rb_im = read(1, rb)
    r_mat = jnp.concatenate([rf_re, rb_re, rf_im, rb_im], axis=1)
    lam = jnp.stack([jnp.concatenate([pw_re[0, t], pw_re[1, t]], axis=-1),
                     jnp.concatenate([pw_im[0, t], pw_im[1, t]], axis=-1)], axis=1)
    gb, nb, p, w = S5_GB, g // S5_GB, S5_STATE, t * S5_CH
    grp = jnp.arange(gb)[:, None, None]
    src = jnp.arange(w)[None, :, None]
    dst = jnp.arange(gb * w)[None, None, :]
    place_tc = (dst == (src // S5_CH) * (gb * S5_CH) + grp * S5_CH + src % S5_CH).astype(F32)
    src = jnp.arange(4 * p)[None, :, None]
    dst = jnp.arange(4 * gb * p)[None, None, :]
    place_st = (dst == (src // p) * (gb * p) + grp * p + src % p).astype(F32)
    pack = lambda left, mat, right: jnp.einsum('gri,ogrs,gsj->oij', left, mat.reshape((nb, gb) + mat.shape[1:]),
                                               right).astype(BF16)
    lam_blk = lam.reshape(nb, gb, 2, 2, p).transpose(0, 2, 3, 1, 4).reshape(nb, 2, 2 * gb * p)
    return (pack(place_tc, m_mat, place_tc), pack(place_tc, e_mat, place_st), pack(place_st, r_mat, place_tc),
            lam_blk)


def _s5_kernel(u_ref, m_ref, e_ref, r_ref, lam_ref, y_ref, d_scr, sp_scr, *, n_ctx_chunks, n_chunks):
    u = u_ref[...]
    d_scr[...] = jnp.dot(u, e_ref[...], preferred_element_type=F32)
    sl = S5_GB * S5_STATE
    lam_re = jnp.broadcast_to(lam_ref[0:1, :], (SUBLANES, 2 * sl))
    lam_im = jnp.broadcast_to(lam_ref[1:2, :], (SUBLANES, 2 * sl))

    def rows(n):
        return pl.ds(pl.multiple_of(n * SUBLANES, SUBLANES), SUBLANES)

    def step(s, carry):
        s_re, s_im = carry
        nf = s
        nb = jnp.where(s < n_ctx_chunks, n_ctx_chunks - 1 - s, n_chunks - 1 - (s - n_ctx_chunks))
        sp_scr[rows(nf), 0:sl] = s_re[:, 0:sl]
        sp_scr[rows(nf), 2 * sl:3 * sl] = s_im[:, 0:sl]
        sp_scr[rows(nb), sl:2 * sl] = s_re[:, sl:2 * sl]
        sp_scr[rows(nb), 3 * sl:4 * sl] = s_im[:, sl:2 * sl]
        d_re = jnp.concatenate([d_scr[rows(nf), 0:sl], d_scr[rows(nb), sl:2 * sl]], axis=1)
        d_im = jnp.concatenate([d_scr[rows(nf), 2 * sl:3 * sl], d_scr[rows(nb), 3 * sl:4 * sl]], axis=1)
        n_re = lam_re * s_re - lam_im * s_im + d_re
        n_im = lam_re * s_im + lam_im * s_re + d_im
        return n_re, n_im

    zero = jnp.zeros((SUBLANES, 2 * sl), F32)
    lax.fori_loop(0, n_chunks, step, (zero, zero))
    y_ref[...] = (jnp.dot(u, m_ref[...], preferred_element_type=F32)
                  + jnp.dot(sp_scr[...].astype(BF16), r_ref[...], preferred_element_type=F32)).astype(y_ref.dtype)


def s5_scan(u_g, m_mat, e_mat, r_mat, lam, n_ctx_chunks, n_chunks):
    nb, r, w = u_g.shape
    sw = e_mat.shape[2]
    return pl.pallas_call(
        functools.partial(_s5_kernel, n_ctx_chunks=n_ctx_chunks, n_chunks=n_chunks),
        grid=(nb,),
        in_specs=[pl.BlockSpec((None, r, w), lambda i: (i, 0, 0)),
                  pl.BlockSpec((None, w, w), lambda i: (i, 0, 0)),
                  pl.BlockSpec((None, w, sw), lambda i: (i, 0, 0)),
                  pl.BlockSpec((None, sw, w), lambda i: (i, 0, 0)),
                  pl.BlockSpec((None, 2, sw // 2), lambda i: (i, 0, 0))],
        out_specs=pl.BlockSpec((None, r, w), lambda i: (i, 0, 0)),
        out_shape=jax.ShapeDtypeStruct((nb, r, w), BF16),
        scratch_shapes=[pltpu.VMEM((r, sw), F32), pltpu.VMEM((r, sw), F32)],
        compiler_params=_cparams("arbitrary"),
        name="s5_scan",
    )(u_g, m_mat, e_mat, r_mat, lam)


def _s5_split_kernel(x_ref, o_ref):
    x = x_ref[...]
    bw = S5_GB * S5_CH
    for q in range(LANES // bw):
        o_ref[q] = jnp.concatenate([x[:, j * LANES + q * bw:j * LANES + (q + 1) * bw] for j in range(S5_T)], axis=1)


def _s5_merge_kernel(y_ref, o_ref):
    bw = S5_GB * S5_CH
    o_ref[...] = jnp.concatenate([y_ref[q][:, j * bw:(j + 1) * bw]
                                  for j in range(S5_T) for q in range(LANES // bw)], axis=1)


def _s5_relayout(x, split):
    k = LANES // (S5_GB * S5_CH)
    wide, narrow = S5_T * LANES, S5_T * S5_GB * S5_CH
    r = x.shape[1]
    row_tile = r // 4 if r % (4 * 2 * SUBLANES) == 0 else r
    no = x.shape[0] if split else x.shape[0] // k
    wide_spec = pl.BlockSpec((None, row_tile, wide), lambda o, i: (o, i, 0))
    narrow_spec = pl.BlockSpec((k, row_tile, narrow), lambda o, i: (o, i, 0))
    return pl.pallas_call(
        _s5_split_kernel if split else _s5_merge_kernel,
        grid=(no, r // row_tile),
        in_specs=[wide_spec if split else narrow_spec],
        out_specs=narrow_spec if split else wide_spec,
        out_shape=jax.ShapeDtypeStruct((no * k, r, narrow) if split else (no, r, wide), x.dtype),
        compiler_params=_cparams("arbitrary", "arbitrary"),
        name="s5_split" if split else "s5_merge",
    )(x)


def s5_layer(u_all, bsz, seq, n_ctx, params):
    m_mat, e_mat, r_mat, lam = s5_group_matrices(*params)
    width = u_all.shape[1]
    no = width // LANES
    t = S5_T

    def to_chunks(u, length):
        u = u.reshape(bsz, length // t, t, no, LANES).transpose(3, 1, 0, 2, 4)
        u = jnp.pad(u, ((0, 0), (0, 0), (0, SUBLANES - bsz), (0, 0), (0, 0)))
        return u.reshape(no, length // t, SUBLANES, t * LANES)

    n_lat = bsz * seq
    ncc, ncl = n_ctx // t, seq // t
    u_o = jnp.concatenate([to_chunks(u_all[n_lat:], n_ctx), to_chunks(u_all[:n_lat], seq)], axis=1)
    u_o = u_o.reshape(no, (ncc + ncl) * SUBLANES, t * LANES).astype(BF16)
    y_g = s5_scan(_s5_relayout(u_o, split=True), m_mat, e_mat, r_mat, lam, ncc, ncc + ncl)
    y_o = _s5_relayout(y_g, split=False).reshape(no, ncc + ncl, SUBLANES, t, LANES)[:, :, :bsz]

    def from_chunks(y, length):
        return y.transpose(2, 1, 3, 0, 4).reshape(bsz * length, width)

    return jnp.concatenate([from_chunks(y_o[:, ncc:], seq), from_chunks(y_o[:, :ncc], n_ctx)], axis=0)


def _glu_kernel(y_ref, w_ref, b_ref, o_ref):
    gl = _gelu(y_ref[...].astype(F32))
    z = jnp.dot(gl.astype(BF16), w_ref[...], preferred_element_type=F32) + b_ref[...]
    o_ref[...] = (gl * jax.nn.sigmoid(z)).astype(o_ref.dtype)


def s5_glu(y, w, b, tm=512):
    m, n = y.shape
    return pl.pallas_call(
        _glu_kernel,
        grid=(m // tm,),
        in_specs=[pl.BlockSpec((tm, n), lambda i: (i, 0)),
                  pl.BlockSpec((n, n), lambda i: (0, 0)),
                  pl.BlockSpec((1, n), lambda i: (0, 0))],
        out_specs=pl.BlockSpec((tm, n), lambda i: (i, 0)),
        out_shape=jax.ShapeDtypeStruct((m, n), BF16),
        compiler_params=_cparams("arbitrary"),
        name="s5_glu",
    )(y, w, b.reshape(1, n))


_TOMB = 2.0 ** 100
_TOMB_GUARD = 2.0 ** 90


def _top_rows(s, n_take, break_ties):
    vals = []
    if not break_ties:
        for k in range(n_take):
            m = jnp.max(s, axis=0, keepdims=True)
            s = jnp.where(s == m, -_TOMB * (1.0 + k / 32.0), s)
            vals.append(m)
        rank = jnp.where(s <= -_TOMB, (s * (-1.0 / _TOMB) - 1.0) * 32.0, NOT_SELECTED)
        return vals, rank, s
    r = s.shape[0]
    rows = lax.broadcasted_iota(jnp.int32, s.shape, 0).astype(F32)
    rank = jnp.full(s.shape, NOT_SELECTED, F32)
    for k in range(n_take):
        m = jnp.max(s, axis=0, keepdims=True)
        idx = jnp.min(jnp.where(s == m, rows, float(r)), axis=0, keepdims=True)
        sel = rows == idx
        rank = jnp.where(sel, float(k), rank)
        s = jnp.where(sel, -jnp.inf, s)
        vals.append(m)
    return vals, rank, s


_CAND_WIDTHS = [PEER_TOPK // (i + 1) for i in range(PEER_TOPK)]
_CAND_STARTS = [sum(_CAND_WIDTHS[:i]) for i in range(PEER_TOPK + 1)]
_CAND_ROWS = -(-_CAND_STARTS[-1] // SUBLANES) * SUBLANES


def _route_head(s1, s2, grp, break_ties):
    v1, rank1, _ = _top_rows(s1, PEER_TOPK, break_ties)
    v2, rank2, _ = _top_rows(s2, PEER_TOPK, break_ties)
    v2m = jnp.concatenate(v2, axis=0)
    pad = jnp.full((_CAND_ROWS - _CAND_STARTS[-1], v2m.shape[1]), -jnp.inf, F32)
    cand = jnp.concatenate([v1[i] + v2m[0:_CAND_WIDTHS[i]] for i in range(PEER_TOPK)] + [pad], axis=0)
    cv, _, left = _top_rows(cand, PEER_TOPK, break_ties)
    picked = jnp.where(left != cand, 1.0, 0.0)
    picked = jnp.concatenate([picked, jnp.zeros((grp.shape[1] - _CAND_ROWS, picked.shape[1]), F32)], axis=0)
    cnt = jnp.dot(grp, picked, preferred_element_type=F32)
    z = jnp.zeros_like(cv[0])
    for k in range(PEER_TOPK):
        z = z + jnp.exp(cv[k] - cv[0])
    thr = jnp.zeros_like(s1)
    for i in range(PEER_TOPK):
        thr = jnp.where(rank1 == float(i), cnt[i:i + 1, :], thr)
    n_out = (jnp.sum(jnp.where(rank1 != NOT_SELECTED, 1.0, 0.0), axis=0, keepdims=True)
             + jnp.sum(jnp.where(rank2 != NOT_SELECTED, 1.0, 0.0), axis=0, keepdims=True)
             + jnp.sum(cnt, axis=0, keepdims=True))
    return thr, jnp.exp(s1 - v1[0]), rank2.astype(BF16), (jnp.exp(s2 - v2[0]) / z).astype(BF16), n_out


def _peer_route_kernel(q_ref, k1_ref, k2_ref, grp_ref, thr_ref, a_ref, r2_ref, b_ref):
    hp = lax.Precision.HIGHEST
    nt = (((1,), (1,)), ((), ()))
    for h in range(PEER_HEADS):
        qh = q_ref[:, h * 2 * PEER_HALF:(h + 1) * 2 * PEER_HALF]
        s1 = lax.dot_general(k1_ref[h], qh, nt, precision=hp, preferred_element_type=F32)
        s2 = lax.dot_general(k2_ref[h], qh, nt, precision=hp, preferred_element_type=F32)

        def write(res, h=h):
            thr_ref[h], a_ref[h], r2_ref[h], b_ref[h] = res[:4]

        fast = _route_head(s1, s2, grp_ref[...], break_ties=False)
        write(fast)

        low = jnp.minimum(jnp.min(s1), jnp.min(s2)) < -_TOMB_GUARD

        @pl.when((jnp.max(fast[4]) > 3.0 * PEER_TOPK) | low)
        def _(s1=s1, s2=s2, write=write):
            write(_route_head(s1, s2, grp_ref[...], break_ties=True))


def peer_route(q, keys, tt=256):
    t = q.shape[0]
    zeros = jnp.zeros((PEER_HEADS, PEER_NKEYS, PEER_HALF), F32)
    k1 = jnp.concatenate([keys[:, 0], zeros], axis=-1)
    k2 = jnp.concatenate([zeros, keys[:, 1]], axis=-1)
    row = jnp.arange(LANES)[None, :]
    grp = ((row >= jnp.array(_CAND_STARTS[:-1])[:, None]) & (row < jnp.array(_CAND_STARTS[1:])[:, None])).astype(F32)
    shp32 = jax.ShapeDtypeStruct((PEER_HEADS, PEER_NKEYS, t), F32)
    shp16 = jax.ShapeDtypeStruct((PEER_HEADS, PEER_NKEYS, t), BF16)
    ospec = pl.BlockSpec((PEER_HEADS, PEER_NKEYS, tt), lambda i: (0, 0, i))
    kspec = pl.BlockSpec((PEER_HEADS, PEER_NKEYS, 2 * PEER_HALF), lambda i: (0, 0, 0))
    return pl.pallas_call(
        _peer_route_kernel,
        grid=(t // tt,),
        in_specs=[pl.BlockSpec((tt, PEER_HEADS * 2 * PEER_HALF), lambda i: (i, 0)), kspec, kspec,
                  pl.BlockSpec((PEER_TOPK, LANES), lambda i: (0, 0))],
        out_specs=[ospec] * 4,
        out_shape=[shp32, shp32, shp16, shp16],
        compiler_params=_cparams("arbitrary"),
        name="peer_route",
    )(q, k1, k2, grp)


def _rows_bf16(row, n_rows):
    packed = jnp.broadcast_to(row, (2 * SUBLANES, row.shape[1])).astype(BF16)
    return jnp.concatenate([packed] * (n_rows // (2 * SUBLANES)), axis=0)


def _peer_dense_kernel(ft_ref, u_ref, vt_ref, thr_ref, a_ref, r2_ref, b_ref, gate_ref, res_ref, ng_ref, o_ref,
                       acc_scr, ge_scr, *, rows_per_tile, final_norm):
    j = pl.program_id(1)

    @pl.when(j == 0)
    def _():
        acc_scr[...] = jnp.zeros_like(acc_scr)
        ge_scr[...] = jnp.zeros_like(ge_scr)

    w_rows = []
    for r in range(rows_per_tile):
        w = None
        for h in range(PEER_HEADS):
            thr = _rows_bf16(thr_ref[h, r:r + 1, :], PEER_NKEYS)
            a = _rows_bf16(a_ref[h, r:r + 1, :], PEER_NKEYS)
            b = b_ref[h]
            term = jnp.where(r2_ref[h] < thr, b, jnp.zeros_like(b)) * a
            w = term if w is None else w + term
        w_rows.append(w * ge_scr[r * PEER_NKEYS:(r + 1) * PEER_NKEYS, :])
    wt = jnp.concatenate(w_rows, axis=0)
    acc_scr[...] += jnp.dot(vt_ref[...], wt, preferred_element_type=F32)
    act = jnp.dot(u_ref[...], ft_ref[...], preferred_element_type=F32)
    ge_scr[...] = _gelu(act).astype(BF16)

    @pl.when(j == pl.num_programs(1) - 1)
    def _():
        hn = res_ref[...] + gate_ref[...] * acc_scr[...].T
        if final_norm:
            ms = jnp.mean(hn * hn, axis=-1, keepdims=True)
            hn = hn * lax.rsqrt(ms + EPS) * ng_ref[...]
        o_ref[...] = hn


def peer_dense(ft, u, vt, layer, thr, a, r2, b, modr, gate_idx, res, mod_row, norm_g=None, tt=512,
               rows_per_tile=8):
    d, t = ft.shape
    e = u.shape[1]
    te = rows_per_tile * PEER_NKEYS
    ne = e // te

    def cur(j):
        return jnp.minimum(j, ne - 1)

    def prev(j):
        return jnp.maximum(j - 1, 0)

    rspec = pl.BlockSpec((PEER_HEADS, rows_per_tile, tt), lambda i, j: (0, prev(j), i))
    cspec = pl.BlockSpec((PEER_HEADS, PEER_NKEYS, tt), lambda i, j: (0, 0, i))
    return pl.pallas_call(
        functools.partial(_peer_dense_kernel, rows_per_tile=rows_per_tile, final_norm=norm_g is not None),
        grid=(t // tt, ne + 1),
        in_specs=[pl.BlockSpec((d, tt), lambda i, j: (0, i)),
                  pl.BlockSpec((None, te, d), lambda i, j: (layer, cur(j), 0)),
                  pl.BlockSpec((None, d, te), lambda i, j: (layer, 0, prev(j))),
                  rspec, rspec, cspec, cspec,
                  pl.BlockSpec((None, 1, d), lambda i, j: (mod_row(i * tt) * N_MOD + gate_idx, 0, 0)),
                  pl.BlockSpec((tt, d), lambda i, j: (i, 0)),
                  pl.BlockSpec((1, d), lambda i, j: (0, 0))],
        out_specs=pl.BlockSpec((tt, d), lambda i, j: (i, 0)),
        out_shape=jax.ShapeDtypeStruct((t, d), F32),
        scratch_shapes=[pltpu.VMEM((d, tt), F32), pltpu.VMEM((te, tt), BF16)],
        compiler_params=_cparams("arbitrary", "arbitrary"),
        name="peer_dense",
    )(ft, u, vt, thr, a, r2, b, modr, res, (jnp.ones((d,), F32) if norm_g is None else norm_g).reshape(1, d))


def peer_ffn(h, modr, mod_row, norm_g, wq, keys, u, vt, layer, tm, final_norm_g=None):
    q, ft = norm_mod_matmul(h, norm_g, modr, 3, 4, wq.astype(BF16), mod_row, tm=tm, tn=512, emit_a=True)
    thr, a, r2, b = peer_route(q, keys)
    return peer_dense(ft, u, vt, layer, thr, a, r2, b, modr, 5, h, mod_row, norm_g=final_norm_g)


def _conv_kernel(x_ref, w_ref, b_ref, o_ref, pad_scr, *, length, row_tile):
    half = SSD_CONV // 2
    zeros = jnp.zeros((SUBLANES, x_ref.shape[1]), F32)
    pad_scr[0:SUBLANES, :] = zeros
    pad_scr[SUBLANES + length:2 * SUBLANES + length, :] = zeros
    pad_scr[SUBLANES:SUBLANES + length, :] = x_ref[...].astype(F32)
    for r0 in range(0, length, row_tile):
        acc = jnp.broadcast_to(b_ref[...], (row_tile, x_ref.shape[1]))
        for k in range(SSD_CONV):
            start = SUBLANES + r0 + k - half
            acc = acc + w_ref[k:k + 1, :] * pad_scr[start:start + row_tile, :]
        o_ref[r0:r0 + row_tile, :] = jax.nn.silu(acc).astype(o_ref.dtype)


def ssd_conv(p, col0, width, row0, n_seq, length, conv_w, conv_b, tc=256):
    row_tile = min(length, 512)
    return pl.pallas_call(
        functools.partial(_conv_kernel, length=length, row_tile=row_tile),
        grid=(n_seq, width // tc),
        in_specs=[pl.BlockSpec((length, tc), lambda s, j: (row0 // length + s, col0 // tc + j)),
                  pl.BlockSpec((SSD_CONV, tc), lambda s, j: (0, j)),
                  pl.BlockSpec((1, tc), lambda s, j: (0, j))],
        out_specs=pl.BlockSpec((length, tc), lambda s, j: (s, j)),
        out_shape=jax.ShapeDtypeStruct((n_seq * length, width), BF16),
        scratch_shapes=[pltpu.VMEM((length + 2 * SUBLANES, tc), F32)],
        compiler_params=_cparams("arbitrary", "arbitrary"),
        name="ssd_conv",
    )(p, conv_w, conv_b.reshape(1, width))


def _ssd_kernel(xl_ref, bl_ref, cl_ref, xx_ref, bx_ref, cx_ref, dt_ref, bias_ref, alog_ref, y_ref, st_scr, *,
                reverse, n_ctx_chunks):
    hp = lax.Precision.HIGHEST
    ch = SSD_CHUNK
    nh = dt_ref.shape[1]
    hpg = nh // SSD_GROUPS
    is_ctx = pl.program_id(1) < n_ctx_chunks

    def pick(ctx_ref, lat_ref, cols):
        return jnp.where(is_ctx, ctx_ref[:, cols], lat_ref[:, cols])

    @pl.when(pl.program_id(1) == 0)
    def _():
        st_scr[...] = jnp.zeros_like(st_scr)

    dt = jax.nn.softplus(dt_ref[...] + bias_ref[...])
    la = dt * (-jnp.exp(alog_ref[...]))
    row = lax.broadcasted_iota(jnp.int32, (ch, ch), 0)
    col = lax.broadcasted_iota(jnp.int32, (ch, ch), 1)
    keep = (col >= row) if reverse else (row >= col)
    tri = jnp.where(keep, 1.0, 0.0)
    cs = jnp.dot(tri, la, precision=hp, preferred_element_type=F32)
    both_t = jnp.concatenate([cs, dt], axis=1).T
    cs_t, dt_t = both_t[0:nh], both_t[nh:2 * nh]
    end = 0 if reverse else ch - 1
    tot_t = cs_t[:, end:end + 1]
    ws_t = jnp.exp(tot_t - cs_t) * dt_t
    dec_t = jnp.exp(tot_t)
    ecs = jnp.exp(cs)
    lo = lax.broadcasted_iota(jnp.int32, (ch, 2 * SSD_HEAD_DIM), 1) < SSD_HEAD_DIM
    nt = (((1,), (1,)), ((), ()))
    for g in range(SSD_GROUPS):
        gs = slice(g * SSD_STATE, (g + 1) * SSD_STATE)
        cg = pick(cx_ref, cl_ref, gs)
        bg = pick(bx_ref, bl_ref, gs)
        cb = lax.dot_general(cg, bg, nt, preferred_element_type=F32)
        bg_t = bg.astype(F32).T
        cg32 = cg.astype(F32)
        for pr in range(hpg // 2):
            lanes = slice((g * hpg + 2 * pr) * SSD_HEAD_DIM, (g * hpg + 2 * pr + 2) * SSD_HEAD_DIM)
            sl = slice(2 * pr * SSD_HEAD_DIM, (2 * pr + 2) * SSD_HEAD_DIM)
            xp = pick(xx_ref, xl_ref, lanes)
            st = st_scr[g, :, sl]
            rhs = jnp.concatenate([xp, st.astype(BF16)], axis=0)
            ys, news, decs = [], [], []
            for j in range(2):
                h = g * hpg + 2 * pr + j
                seg = cs[:, h:h + 1] - cs_t[h:h + 1, :]
                gmat = jnp.where(keep, jnp.exp(seg), 0.0) * cb * dt_t[h:h + 1, :]
                cmat = cg32 * ecs[:, h:h + 1]
                lhs = jnp.concatenate([gmat, cmat], axis=1).astype(BF16)
                ys.append(jnp.dot(lhs, rhs, preferred_element_type=F32))
                bw = (bg_t * ws_t[h:h + 1, :]).astype(BF16)
                news.append(jnp.dot(bw, xp, preferred_element_type=F32))
                decs.append(jnp.broadcast_to(dec_t[h:h + 1, :], (ch, 2 * SSD_HEAD_DIM)))
            y_ref[:, lanes] = jnp.where(lo, ys[0], ys[1]).astype(y_ref.dtype)
            st_scr[g, :, sl] = st * jnp.where(lo, decs[0], decs[1]) + jnp.where(lo, news[0], news[1])


def ssd_scan(xc_lat, xc_ctx, dt_raw, dt_bias, a_log, bsz, seq, n_ctx, reverse):
    nh = dt_raw.shape[1]
    inner = nh * SSD_HEAD_DIM
    gn = SSD_GROUPS * SSD_STATE
    nl, nc = seq // SSD_CHUNK, n_ctx // SSD_CHUNK
    lat_blocks = bsz * nl

    def lblk(b, s):
        sl = jnp.maximum(s - nc, 0)
        return b * nl + ((nl - 1 - sl) if reverse else sl)

    def cblk(b, s):
        sc = jnp.minimum(s, nc - 1)
        return b * nc + ((nc - 1 - sc) if reverse else sc)

    def blk(b, s):
        return jnp.where(s < nc, lat_blocks + cblk(b, s), lblk(b, s))

    def xbc_specs(fn):
        return [pl.BlockSpec((SSD_CHUNK, inner), lambda b, s: (fn(b, s), 0)),
                pl.BlockSpec((SSD_CHUNK, gn), lambda b, s: (fn(b, s), inner // gn)),
                pl.BlockSpec((SSD_CHUNK, gn), lambda b, s: (fn(b, s), inner // gn + 1))]

    return pl.pallas_call(
        functools.partial(_ssd_kernel, reverse=reverse, n_ctx_chunks=nc),
        grid=(bsz, nc + nl),
        in_specs=xbc_specs(lblk) + xbc_specs(cblk) + [
            pl.BlockSpec((SSD_CHUNK, nh), lambda b, s: (blk(b, s), 0)),
            pl.BlockSpec((1, nh), lambda b, s: (0, 0)),
            pl.BlockSpec((1, nh), lambda b, s: (0, 0))],
        out_specs=pl.BlockSpec((SSD_CHUNK, inner), lambda b, s: (lblk(b, s), 0)),
        out_shape=jax.ShapeDtypeStruct((bsz * seq, inner), BF16),
        scratch_shapes=[pltpu.VMEM((SSD_GROUPS, SSD_STATE, inner // SSD_GROUPS), F32)],
        compiler_params=_cparams("arbitrary", "arbitrary"),
        name="ssd_scan_bwd" if reverse else "ssd_scan_fwd",
    )(xc_lat, xc_lat, xc_lat, xc_ctx, xc_ctx, xc_ctx, dt_raw, dt_bias.reshape(1, nh), a_log.reshape(1, nh))


def _ssd_out_kernel(yf_ref, yb_ref, x_ref, z_ref, d_ref, ng_ref, w_ref, gate_ref, res_ref, o_ref, a_scr):
    @pl.when(pl.program_id(1) == 0)
    def _():
        y = (x_ref[...].astype(F32) * d_ref[...] + yf_ref[...].astype(F32) + yb_ref[...].astype(F32))
        gv = y * jax.nn.silu(z_ref[...].astype(F32))
        gw = gv.shape[1] // SSD_GROUPS
        for g in range(SSD_GROUPS):
            part = gv[:, g * gw:(g + 1) * gw]
            ms = jnp.mean(part * part, axis=-1, keepdims=True)
            a_scr[:, g * gw:(g + 1) * gw] = (part * lax.rsqrt(ms + EPS) * ng_ref[:, g * gw:(g + 1) * gw]).astype(BF16)

    acc = jnp.dot(a_scr[...], w_ref[...], preferred_element_type=F32)
    o_ref[...] = res_ref[...] + gate_ref[...] * acc


def ssd_output(yf, yb, xc, pz, d_exp, norm_g, w_out, modr, res, mod_row, tm=512, tn=512):
    m, inner = yf.shape
    n = w_out.shape[1]
    return pl.pallas_call(
        _ssd_out_kernel,
        grid=(m // tm, n // tn),
        in_specs=[pl.BlockSpec((tm, inner), lambda i, j: (i, 0)),
                  pl.BlockSpec((tm, inner), lambda i, j: (i, 0)),
                  pl.BlockSpec((tm, inner), lambda i, j: (i, 0)),
                  pl.BlockSpec((tm, inner), lambda i, j: (i, 0)),
                  pl.BlockSpec((1, inner), lambda i, j: (0, 0)),
                  pl.BlockSpec((1, inner), lambda i, j: (0, 0)),
                  pl.BlockSpec((inner, tn), lambda i, j: (0, j)),
                  pl.BlockSpec((None, 1, tn), lambda i, j: (mod_row(i * tm) * N_MOD + 2, 0, j)),
                  pl.BlockSpec((tm, tn), lambda i, j: (i, j))],
        out_specs=pl.BlockSpec((tm, tn), lambda i, j: (i, j)),
        out_shape=jax.ShapeDtypeStruct((m, n), F32),
        scratch_shapes=[pltpu.VMEM((tm, inner), BF16)],
        compiler_params=_cparams("arbitrary", "arbitrary"),
        name="ssd_output",
    )(yf, yb, xc, pz, d_exp.reshape(1, inner), norm_g.reshape(1, inner), w_out, modr, res)


def _mod_rows(c, c_ctx, w, b, layer):
    bsz, d = c.shape
    cond = jnp.zeros((SUBLANES, d), F32).at[:bsz].set(c).at[bsz].set(c_ctx)
    mod = modulation(cond, w, b, layer)
    return mod.reshape(SUBLANES * N_MOD, 1, d)


def attn_s5_layer(h, bsz, seq, n_ctx, modr, mod_row, tm, norm1_g, w_in, sink, s5_params, glu_w, glu_b, w_out):
    n_lat = bsz * seq
    qw = ATTN_HEADS * HEAD_DIM
    kvw = ATTN_KV_HEADS * HEAD_DIM
    p = norm_mod_matmul(h, norm1_g, modr, 0, 1, w_in.astype(BF16), mod_row, tm=tm, tn=512, out_dtype=BF16)
    attn_lat = attention_latent(p, sink, bsz, seq, n_ctx)
    attn_ctx = attention_context(p, sink, bsz, seq, n_ctx)
    attn = jnp.concatenate([attn_lat, attn_ctx], axis=0)
    y = s5_layer(p[:, qw + 2 * kvw:], bsz, seq, n_ctx, s5_params)
    s5o = s5_glu(y, glu_w.astype(BF16), glu_b)
    w_out = w_out.astype(BF16)
    return matmul_gated_residual([attn, s5o], [w_out[:qw], w_out[qw:]], modr, 2, h, mod_row, tm=tm, tn=512)


def ssd_layer(h, bsz, seq, n_ctx, modr, mod_row, tm, norm1_g, w_in, conv_w, conv_b, dt_bias, a_log, d_skip,
              norm_g, w_out):
    n_lat = bsz * seq
    nh = a_log.shape[1]
    inner = nh * SSD_HEAD_DIM
    xbc_w = inner + 2 * SSD_GROUPS * SSD_STATE
    w_in = w_in.astype(BF16)
    pz = norm_mod_matmul(h, norm1_g, modr, 0, 1, w_in[:, :inner + xbc_w], mod_row, tm=tm, tn=1024,
                         out_dtype=BF16)
    dt_raw = norm_mod_matmul(h, norm1_g, modr, 0, 1, w_in[:, inner + xbc_w:], mod_row, tm=tm, tn=2 * nh)
    xc = ssd_conv(pz, inner, xbc_w, 0, bsz, seq, conv_w, conv_b)
    xc_ctx = ssd_conv(pz, inner, xbc_w, n_lat, bsz, n_ctx, conv_w, conv_b)
    yf = ssd_scan(xc, xc_ctx, dt_raw[:, :nh], dt_bias[0], a_log[0], bsz, seq, n_ctx, reverse=False)
    yb = ssd_scan(xc, xc_ctx, dt_raw[:, nh:], dt_bias[1], a_log[1], bsz, seq, n_ctx, reverse=True)
    d_exp = jnp.repeat(d_skip, SSD_HEAD_DIM)
    return ssd_output(yf, yb, xc, pz, d_exp, norm_g, w_out.astype(BF16), modr, h, mod_row)


def kernel(x, c, ctx, c_ctx, mod_w, mod_b, norm1_g, norm2_g, ab_w_in, attn_sink, s5_a_re, s5_a_im, s5_log_dt, s5_b_re, s5_b_im, s5_c_re, s5_c_im, s5_d, s5_glu_w, s5_glu_b, ab_w_out, ssd_w_in, ssd_conv_w, ssd_conv_b, ssd_dt_bias, ssd_a_log, ssd_d, ssd_norm_g, ssd_w_out, peer_wq, peer_keys, peer_u, peer_v, final_norm_g):
    bsz, seq, d = x.shape
    n_ctx = ctx.shape[1]
    n_lat = bsz * seq
    depth = mod_w.shape[0]
    tm = 512

    def mod_row(r):
        return jnp.minimum(r // seq, bsz)

    h = jnp.concatenate([x.reshape(n_lat, d), ctx.reshape(bsz * n_ctx, d)], axis=0)
    peer_u16 = peer_u.astype(BF16)
    peer_vt16 = peer_v.transpose(0, 2, 1).astype(BF16)
    for i in range(depth):
        last = i == depth - 1
        j = i // 2
        modr = _mod_rows(c, c_ctx, mod_w, mod_b, i)
        if i % 2 == 0:
            s5_params = (s5_a_re[j], s5_a_im[j], s5_log_dt[j], s5_b_re[j], s5_b_im[j], s5_c_re[j], s5_c_im[j],
                         s5_d[j])
            h = attn_s5_layer(h, bsz, seq, n_ctx, modr, mod_row, tm, norm1_g[i], ab_w_in[j], attn_sink[j],
                              s5_params, s5_glu_w[j], s5_glu_b[j], ab_w_out[j])
        else:
            if not last:
                raise NotImplementedError("an SSD layer that must also update the context stream")
            h = ssd_layer(h, bsz, seq, n_ctx, modr, mod_row, tm, norm1_g[i], ssd_w_in[j], ssd_conv_w[j],
                          ssd_conv_b[j], ssd_dt_bias[j], ssd_a_log[j], ssd_d[j], ssd_norm_g[j], ssd_w_out[j])
        h = peer_ffn(h, modr, mod_row, norm2_g[i], peer_wq[i], peer_keys[i], peer_u16, peer_vt16, i, tm,
                     final_norm_g=final_norm_g if last else None)
    return h[:n_lat].reshape(bsz, seq, d)
```

```python
import functools
import math

import jax
import jax.numpy as jnp
from jax import lax
from jax.experimental import pallas as pl
from jax.experimental.pallas import tpu as pltpu

F32 = jnp.float32
BF16 = jnp.bfloat16
EPS = 1e-6

V7X_VMEM_LIMIT_BYTES = 56 * 1024 * 1024
LANES = 128
SUBLANES = 8

N_MOD = 6
HEAD_DIM = 128
ATTN_HEADS = 8
ATTN_KV_HEADS = 2
ATTN_GROUP = ATTN_HEADS // ATTN_KV_HEADS
ATTN_SCALE = HEAD_DIM ** -0.5
BLOCK = 128
GRID_W = 64
ROPE_THETA = 10000.0
S5_CH = 16
S5_STATE = 64
S5_T = 16
S5_GB = 2
SSD_HEAD_DIM = 64
SSD_GROUPS = 8
SSD_STATE = 128
SSD_CONV = 5
SSD_CHUNK = 128
PEER_HEADS = 8
PEER_NKEYS = 128
PEER_HALF = 64
PEER_TOPK = 16
NOT_SELECTED = 127.0


def _cparams(*sem):
    return pltpu.CompilerParams(dimension_semantics=sem, vmem_limit_bytes=V7X_VMEM_LIMIT_BYTES)


def _gelu(x):
    return 0.5 * x * (1.0 + lax.erf(x * (2.0 ** -0.5)))


def _mod_kernel(c_ref, w_ref, b_ref, o_ref):
    s = jax.nn.silu(c_ref[...]).astype(BF16)
    o_ref[...] = jnp.dot(s, w_ref[...].astype(BF16), preferred_element_type=F32) + b_ref[...]


def modulation(cond, w, b, layer, tn=1024):
    rows, d = cond.shape
    n = w.shape[2]
    return pl.pallas_call(
        _mod_kernel,
        grid=(n // tn,),
        in_specs=[pl.BlockSpec((rows, d), lambda j: (0, 0)),
                  pl.BlockSpec((None, d, tn), lambda j: (layer, 0, j)),
                  pl.BlockSpec((None, 1, tn), lambda j: (layer, 0, j))],
        out_specs=pl.BlockSpec((rows, tn), lambda j: (0, j)),
        out_shape=jax.ShapeDtypeStruct((rows, n), F32),
        compiler_params=_cparams("arbitrary"),
        name="modulation",
    )(cond, w, b.reshape(b.shape[0], 1, n))


def _nmm_kernel(x_ref, g_ref, sh_ref, sc_ref, w_ref, *refs, emit_a):
    if emit_a:
        o_ref, a_out_ref, a_scr = refs
    else:
        o_ref, a_scr = refs

    @pl.when(pl.program_id(1) == 0)
    def _():
        x = x_ref[...]
        ms = jnp.mean(x * x, axis=-1, keepdims=True)
        y = x * lax.rsqrt(ms + EPS) * g_ref[...]
        a = y * (1.0 + sc_ref[...]) + sh_ref[...]
        a_scr[...] = a.astype(BF16)
        if emit_a:
            a_out_ref[...] = a.T.astype(BF16)

    o_ref[...] = jnp.dot(a_scr[...], w_ref[...], preferred_element_type=F32).astype(o_ref.dtype)


def norm_mod_matmul(x, g, modr, shift_idx, scale_idx, w, mod_row, tm, tn, out_dtype=F32, emit_a=False):
    m, d = x.shape
    n = w.shape[1]
    out_shape = [jax.ShapeDtypeStruct((m, n), out_dtype)]
    out_specs = [pl.BlockSpec((tm, tn), lambda i, j: (i, j))]
    if emit_a:
        out_shape.append(jax.ShapeDtypeStruct((d, m), BF16))
        out_specs.append(pl.BlockSpec((d, tm), lambda i, j: (0, i)))
    res = pl.pallas_call(
        functools.partial(_nmm_kernel, emit_a=emit_a),
        grid=(m // tm, n // tn),
        in_specs=[pl.BlockSpec((tm, d), lambda i, j: (i, 0)),
                  pl.BlockSpec((1, d), lambda i, j: (0, 0)),
                  pl.BlockSpec((None, 1, d), lambda i, j: (mod_row(i * tm) * N_MOD + shift_idx, 0, 0)),
                  pl.BlockSpec((None, 1, d), lambda i, j: (mod_row(i * tm) * N_MOD + scale_idx, 0, 0)),
                  pl.BlockSpec((d, tn), lambda i, j: (0, j))],
        out_specs=out_specs,
        out_shape=out_shape,
        scratch_shapes=[pltpu.VMEM((tm, d), BF16)],
        compiler_params=_cparams("arbitrary", "arbitrary"),
        name="norm_mod_matmul",
    )(x, g.reshape(1, d), modr, modr, w)
    return res if emit_a else res[0]


def _mmres_kernel(*refs, n_parts):
    xs = refs[:n_parts]
    ws = refs[n_parts:2 * n_parts]
    gate_ref, res_ref, o_ref = refs[2 * n_parts:]
    acc = jnp.dot(xs[0][...], ws[0][...], preferred_element_type=F32)
    for p in range(1, n_parts):
        acc = acc + jnp.dot(xs[p][...], ws[p][...], preferred_element_type=F32)
    o_ref[...] = res_ref[...] + gate_ref[...] * acc


def matmul_gated_residual(xs, ws, modr, gate_idx, res, mod_row, tm, tn):
    m, n = res.shape
    n_parts = len(xs)
    in_specs = ([pl.BlockSpec((tm, x.shape[1]), lambda i, j: (i, 0)) for x in xs]
                + [pl.BlockSpec((w.shape[0], tn), lambda i, j: (0, j)) for w in ws]
                + [pl.BlockSpec((None, 1, tn), lambda i, j: (mod_row(i * tm) * N_MOD + gate_idx, 0, j)),
                   pl.BlockSpec((tm, tn), lambda i, j: (i, j))])
    return pl.pallas_call(
        functools.partial(_mmres_kernel, n_parts=n_parts),
        grid=(m // tm, n // tn),
        in_specs=in_specs,
        out_specs=pl.BlockSpec((tm, tn), lambda i, j: (i, j)),
        out_shape=jax.ShapeDtypeStruct((m, n), F32),
        compiler_params=_cparams("arbitrary", "arbitrary"),
        name="matmul_gated_residual",
    )(*xs, *ws, modr, res)


def rope_tables(seq):
    t = jnp.arange(seq)
    row = (t // GRID_W).astype(F32)
    col = (t % GRID_W).astype(F32)
    n_freq = HEAD_DIM // 4
    inv = ROPE_THETA ** (-jnp.arange(n_freq, dtype=F32) / n_freq)
    ang = jnp.concatenate([row[:, None] * inv, col[:, None] * inv], axis=-1)
    cos, sin = jnp.cos(ang), jnp.sin(ang)
    return jnp.concatenate([cos, cos], axis=-1), jnp.concatenate([-sin, sin], axis=-1)


def _attn_group(q, k, v, mask, sink_col):
    s = lax.dot_general(q, k, (((1,), (1,)), ((), ())), preferred_element_type=F32) * ATTN_SCALE
    if mask is not None:
        s = jnp.where(mask, s, -jnp.inf)
    m = jnp.maximum(jnp.max(s, axis=-1, keepdims=True), sink_col)
    e = jnp.exp(s - m)
    den = jnp.sum(e, axis=-1, keepdims=True) + jnp.exp(sink_col - m)
    o = jnp.dot(e.astype(BF16), v, preferred_element_type=F32)
    return o / den


def _sink_col(sink_ref, hk):
    return jnp.concatenate([jnp.full((BLOCK, 1), sink_ref[hk * ATTN_GROUP + g], F32)
                            for g in range(ATTN_GROUP)], axis=0)


def _rope(x, cos_ref, sin_ref):
    x = x.astype(F32)
    return (x * cos_ref[...] + pltpu.roll(x, HEAD_DIM // 2, 1) * sin_ref[...]).astype(BF16)


def _attn_lat_kernel(sink_ref, q_ref, kp_ref, kc_ref, kn_ref, vp_ref, vc_ref, vn_ref, kx_ref, vx_ref,
                     cp_ref, cc_ref, cn_ref, sp_ref, sc_ref, sn_ref, o_ref, *, n_blocks, n_ctx):
    n = pl.program_id(1)
    row = lax.broadcasted_iota(jnp.int32, (BLOCK, BLOCK), 0)
    col = lax.broadcasted_iota(jnp.int32, (BLOCK, BLOCK), 1)
    m_prev = (col >= row) & (n > 0)
    m_next = (col <= row) & (n < n_blocks - 1)
    ones_ctx = jnp.ones((BLOCK, n_ctx), jnp.bool_)
    ones_cur = jnp.ones((BLOCK, BLOCK), jnp.bool_)
    mask1 = jnp.concatenate([ones_ctx, m_prev, ones_cur, m_next], axis=1)
    mask = jnp.concatenate([mask1] * ATTN_GROUP, axis=0)
    for hk in range(ATTN_KV_HEADS):
        hs = slice(hk * HEAD_DIM, (hk + 1) * HEAD_DIM)
        k = jnp.concatenate([kx_ref[:, hs].astype(BF16), _rope(kp_ref[:, hs], cp_ref, sp_ref),
                             _rope(kc_ref[:, hs], cc_ref, sc_ref), _rope(kn_ref[:, hs], cn_ref, sn_ref)], axis=0)
        v = jnp.concatenate([vx_ref[:, hs], vp_ref[:, hs], vc_ref[:, hs], vn_ref[:, hs]], axis=0).astype(BF16)
        q = jnp.concatenate([_rope(q_ref[:, (hk * ATTN_GROUP + g) * HEAD_DIM:(hk * ATTN_GROUP + g + 1) * HEAD_DIM],
                                   cc_ref, sc_ref) for g in range(ATTN_GROUP)], axis=0)
        o = _attn_group(q, k, v, mask, _sink_col(sink_ref, hk))
        for g in range(ATTN_GROUP):
            h = hk * ATTN_GROUP + g
            o_ref[:, h * HEAD_DIM:(h + 1) * HEAD_DIM] = o[g * BLOCK:(g + 1) * BLOCK].astype(o_ref.dtype)


def attention_latent(p, sink, bsz, seq, n_ctx):
    cos, sin = rope_tables(seq)
    nb = seq // BLOCK
    kvw = ATTN_KV_HEADS * HEAD_DIM
    qw = ATTN_HEADS * HEAD_DIM
    kcol, vcol = qw // kvw, (qw + kvw) // kvw
    ctx_blk0 = bsz * seq // n_ctx

    def prev(b, n):
        return b * nb + jnp.maximum(n - 1, 0)

    def nxt(b, n):
        return b * nb + jnp.minimum(n + 1, nb - 1)

    return pl.pallas_call(
        functools.partial(_attn_lat_kernel, n_blocks=nb, n_ctx=n_ctx),
        grid=(bsz, nb),
        in_specs=[pl.BlockSpec(memory_space=pltpu.SMEM),
                  pl.BlockSpec((BLOCK, qw), lambda b, n: (b * nb + n, 0)),
                  pl.BlockSpec((BLOCK, kvw), lambda b, n: (prev(b, n), kcol)),
                  pl.BlockSpec((BLOCK, kvw), lambda b, n: (b * nb + n, kcol)),
                  pl.BlockSpec((BLOCK, kvw), lambda b, n: (nxt(b, n), kcol)),
                  pl.BlockSpec((BLOCK, kvw), lambda b, n: (prev(b, n), vcol)),
                  pl.BlockSpec((BLOCK, kvw), lambda b, n: (b * nb + n, vcol)),
                  pl.BlockSpec((BLOCK, kvw), lambda b, n: (nxt(b, n), vcol)),
                  pl.BlockSpec((n_ctx, kvw), lambda b, n: (ctx_blk0 + b, kcol)),
                  pl.BlockSpec((n_ctx, kvw), lambda b, n: (ctx_blk0 + b, vcol))]
        + [pl.BlockSpec((BLOCK, HEAD_DIM), fn) for fn in (
            lambda b, n: (jnp.maximum(n - 1, 0), 0), lambda b, n: (n, 0),
            lambda b, n: (jnp.minimum(n + 1, nb - 1), 0))] * 2,
        out_specs=pl.BlockSpec((BLOCK, qw), lambda b, n: (b * nb + n, 0)),
        out_shape=jax.ShapeDtypeStruct((bsz * seq, qw), BF16),
        compiler_params=_cparams("arbitrary", "arbitrary"),
        name="attention_latent",
    )(sink, p, p, p, p, p, p, p, p, p, cos, cos, cos, sin, sin, sin)


def _attn_ctx_kernel(sink_ref, q_ref, k_ref, v_ref, o_ref, *, n_ctx):
    for hk in range(ATTN_KV_HEADS):
        hs = slice(hk * HEAD_DIM, (hk + 1) * HEAD_DIM)
        k = k_ref[:, hs].astype(BF16)
        v = v_ref[:, hs].astype(BF16)
        for g in range(ATTN_GROUP):
            h = hk * ATTN_GROUP + g
            q = q_ref[:, h * HEAD_DIM:(h + 1) * HEAD_DIM].astype(BF16)
            sink_col = jnp.full((n_ctx, 1), sink_ref[h], F32)
            o = _attn_group(q, k, v, None, sink_col)
            o_ref[:, h * HEAD_DIM:(h + 1) * HEAD_DIM] = o.astype(o_ref.dtype)


def attention_context(p, sink, bsz, seq, n_ctx):
    kvw = ATTN_KV_HEADS * HEAD_DIM
    qw = ATTN_HEADS * HEAD_DIM
    kcol, vcol = qw // kvw, (qw + kvw) // kvw
    ctx_blk0 = bsz * seq // n_ctx
    return pl.pallas_call(
        functools.partial(_attn_ctx_kernel, n_ctx=n_ctx),
        grid=(bsz,),
        in_specs=[pl.BlockSpec(memory_space=pltpu.SMEM),
                  pl.BlockSpec((n_ctx, qw), lambda b: (ctx_blk0 + b, 0)),
                  pl.BlockSpec((n_ctx, kvw), lambda b: (ctx_blk0 + b, kcol)),
                  pl.BlockSpec((n_ctx, kvw), lambda b: (ctx_blk0 + b, vcol))],
        out_specs=pl.BlockSpec((n_ctx, qw), lambda b: (b, 0)),
        out_shape=jax.ShapeDtypeStruct((bsz * n_ctx, qw), BF16),
        compiler_params=_cparams("arbitrary"),
        name="attention_context",
    )(sink, p, p, p)


def _cmul(ar, ai, br, bi):
    return ar * br - ai * bi, ar * bi + ai * br


def s5_group_matrices(a_re, a_im, log_dt, b_re, b_im, c_re, c_im, d):
    hp = lax.Precision.HIGHEST
    t = S5_T
    dt = jnp.exp(log_dt)
    mag = jnp.exp(dt * a_re)
    lr, li = mag * jnp.cos(dt * a_im), mag * jnp.sin(dt * a_im)
    den = a_re * a_re + a_im * a_im
    f_re = ((lr - 1) * a_re + li * a_im) / den
    f_im = (li * a_re - (lr - 1) * a_im) / den
    bb_re = f_re[..., None] * b_re - f_im[..., None] * b_im
    bb_im = f_re[..., None] * b_im + f_im[..., None] * b_re
    pr, pi = [jnp.ones_like(lr)], [jnp.zeros_like(li)]
    for _ in range(t):
        nr, ni = _cmul(pr[-1], pi[-1], lr, li)
        pr.append(nr)
        pi.append(ni)
    pw_re, pw_im = jnp.stack(pr, axis=1), jnp.stack(pi, axis=1)
    cl_re = c_re[:, None] * pw_re[:, :, :, None, :] - c_im[:, None] * pw_im[:, :, :, None, :]
    cl_im = c_re[:, None] * pw_im[:, :, :, None, :] + c_im[:, None] * pw_re[:, :, :, None, :]
    kern = jnp.einsum('dkgcp,dgpe->dkgce', jnp.concatenate([cl_re, -cl_im], axis=-1),
                      jnp.concatenate([bb_re, bb_im], axis=2), precision=hp)
    g = a_re.shape[1]
    ii = jnp.arange(t)[:, None]
    jj = jnp.arange(t)[None, :]
    kf = kern[0][jnp.clip(jj - ii, 0, t)]
    kb = kern[1][jnp.clip(ii - jj, 0, t)]
    mf = (ii <= jj)[:, :, None, None, None]
    mb = (ii >= jj)[:, :, None, None, None]
    eye = (ii == jj)[:, :, None, None, None] * jnp.eye(S5_CH, dtype=F32)[None, None, None] * d[None, None, :, :, None]
    mm = jnp.where(mf, kf, 0.0) + jnp.where(mb, kb, 0.0) + eye
    m_mat = mm.transpose(2, 0, 4, 1, 3).reshape(g, t * S5_CH, t * S5_CH)
    pf = jnp.arange(t - 1, -1, -1)
    pb = jnp.arange(t)

    def drive(dirn, pidx):
        er, ei = _cmul(pw_re[dirn][pidx][:, :, :, None], pw_im[dirn][pidx][:, :, :, None],
                       bb_re[dirn][None], bb_im[dirn][None])
        er = er.transpose(1, 0, 3, 2).reshape(g, t * S5_CH, S5_STATE)
        ei = ei.transpose(1, 0, 3, 2).reshape(g, t * S5_CH, S5_STATE)
        return er, ei

    ef_re, ef_im = drive(0, pf)
    eb_re, eb_im = drive(1, pb)
    e_mat = jnp.concatenate([ef_re, eb_re, ef_im, eb_im], axis=-1)
    rf = jnp.arange(1, t + 1)
    rb = jnp.arange(t, 0, -1)

    def read(dirn, pidx):
        rr = cl_re[dirn][pidx]
        ri = cl_im[dirn][pidx]
        rr = rr.transpose(1, 3, 0, 2).reshape(g, S5_STATE, t * S5_CH)
        ri = ri.transpose(1, 3, 0, 2).reshape(g, S5_STATE, t * S5_CH)
        return rr, -ri

    rf_re, rf_im = read(0, rf)
    rb_re, rb_im = read(1, rb)
    r_mat = jnp.concatenate([rf_re, rb_re, rf_im, rb_im], axis=1)
    lam = jnp.stack([jnp.concatenate([pw_re[0, t], pw_re[1, t]], axis=-1),
                     jnp.concatenate([pw_im[0, t], pw_im[1, t]], axis=-1)], axis=1)
    gb, nb, p, w = S5_GB, g // S5_GB, S5_STATE, t * S5_CH
    grp = jnp.arange(gb)[:, None, None]
    src = jnp.arange(w)[None, :, None]
    dst = jnp.arange(gb * w)[None, None, :]
    place_tc = (dst == (src // S5_CH) * (gb * S5_CH) + grp * S5_CH + src % S5_CH).astype(F32)
    src = jnp.arange(4 * p)[None, :, None]
    dst = jnp.arange(4 * gb * p)[None, None, :]
    place_st = (dst == (src // p) * (gb * p) + grp * p + src % p).astype(F32)
    pack = lambda left, mat, right: jnp.einsum('gri,ogrs,gsj->oij', left, mat.reshape((nb, gb) + mat.shape[1:]),
                                               right).astype(BF16)
    lam_blk = lam.reshape(nb, gb, 2, 2, p).transpose(0, 2, 3, 1, 4).reshape(nb, 2, 2 * gb * p)
    return (pack(place_tc, m_mat, place_tc), pack(place_tc, e_mat, place_st), pack(place_st, r_mat, place_tc),
            lam_blk)


def _s5_kernel(u_ref, m_ref, e_ref, r_ref, lam_ref, y_ref, d_scr, sp_scr, *, n_ctx_chunks, n_chunks):
    u = u_ref[...]
    d_scr[...] = jnp.dot(u, e_ref[...], preferred_element_type=F32)
    sl = S5_GB * S5_STATE
    lam_re = jnp.broadcast_to(lam_ref[0:1, :], (SUBLANES, 2 * sl))
    lam_im = jnp.broadcast_to(lam_ref[1:2, :], (SUBLANES, 2 * sl))

    def rows(n):
        return pl.ds(pl.multiple_of(n * SUBLANES, SUBLANES), SUBLANES)

    def step(s, carry):
        s_re, s_im = carry
        nf = s
        nb = jnp.where(s < n_ctx_chunks, n_ctx_chunks - 1 - s, n_chunks - 1 - (s - n_ctx_chunks))
        sp_scr[rows(nf), 0:sl] = s_re[:, 0:sl]
        sp_scr[rows(nf), 2 * sl:3 * sl] = s_im[:, 0:sl]
        sp_scr[rows(nb), sl:2 * sl] = s_re[:, sl:2 * sl]
        sp_scr[rows(nb), 3 * sl:4 * sl] = s_im[:, sl:2 * sl]
        d_re = jnp.concatenate([d_scr[rows(nf), 0:sl], d_scr[rows(nb), sl:2 * sl]], axis=1)
        d_im = jnp.concatenate([d_scr[rows(nf), 2 * sl:3 * sl], d_scr[rows(nb), 3 * sl:4 * sl]], axis=1)
        n_re = lam_re * s_re - lam_im * s_im + d_re
        n_im = lam_re * s_im + lam_im * s_re + d_im
        return n_re, n_im

    zero = jnp.zeros((SUBLANES, 2 * sl), F32)
    lax.fori_loop(0, n_chunks, step, (zero, zero))
    y_ref[...] = (jnp.dot(u, m_ref[...], preferred_element_type=F32)
                  + jnp.dot(sp_scr[...].astype(BF16), r_ref[...], preferred_element_type=F32)).astype(y_ref.dtype)


def s5_scan(u_g, m_mat, e_mat, r_mat, lam, n_ctx_chunks, n_chunks):
    nb, r, w = u_g.shape
    sw = e_mat.shape[2]
    return pl.pallas_call(
        functools.partial(_s5_kernel, n_ctx_chunks=n_ctx_chunks, n_chunks=n_chunks),
        grid=(nb,),
        in_specs=[pl.BlockSpec((None, r, w), lambda i: (i, 0, 0)),
                  pl.BlockSpec((None, w, w), lambda i: (i, 0, 0)),
                  pl.BlockSpec((None, w, sw), lambda i: (i, 0, 0)),
                  pl.BlockSpec((None, sw, w), lambda i: (i, 0, 0)),
                  pl.BlockSpec((None, 2, sw // 2), lambda i: (i, 0, 0))],
        out_specs=pl.BlockSpec((None, r, w), lambda i: (i, 0, 0)),
        out_shape=jax.ShapeDtypeStruct((nb, r, w), BF16),
        scratch_shapes=[pltpu.VMEM((r, sw), F32), pltpu.VMEM((r, sw), F32)],
        compiler_params=_cparams("arbitrary"),
        name="s5_scan",
    )(u_g, m_mat, e_mat, r_mat, lam)


def _s5_split_kernel(x_ref, o_ref):
    x = x_ref[...]
    bw = S5_GB * S5_CH
    for q in range(LANES // bw):
        o_ref[q] = jnp.concatenate([x[:, j * LANES + q * bw:j * LANES + (q + 1) * bw] for j in range(S5_T)], axis=1)


def _s5_merge_kernel(y_ref, o_ref):
    bw = S5_GB * S5_CH
    o_ref[...] = jnp.concatenate([y_ref[q][:, j * bw:(j + 1) * bw]
                                  for j in range(S5_T) for q in range(LANES // bw)], axis=1)


def _s5_relayout(x, split):
    k = LANES // (S5_GB * S5_CH)
    wide, narrow = S5_T * LANES, S5_T * S5_GB * S5_CH
    r = x.shape[1]
    row_tile = r // 4 if r % (4 * 2 * SUBLANES) == 0 else r
    no = x.shape[0] if split else x.shape[0] // k
    wide_spec = pl.BlockSpec((None, row_tile, wide), lambda o, i: (o, i, 0))
    narrow_spec = pl.BlockSpec((k, row_tile, narrow), lambda o, i: (o, i, 0))
    return pl.pallas_call(
        _s5_split_kernel if split else _s5_merge_kernel,
        grid=(no, r // row_tile),
        in_specs=[wide_spec if split else narrow_spec],
        out_specs=narrow_spec if split else wide_spec,
        out_shape=jax.ShapeDtypeStruct((no * k, r, narrow) if split else (no, r, wide), x.dtype),
        compiler_params=_cparams("arbitrary", "arbitrary"),
        name="s5_split" if split else "s5_merge",
    )(x)


def s5_layer(u_all, bsz, seq, n_ctx, params):
    m_mat, e_mat, r_mat, lam = s5_group_matrices(*params)
    width = u_all.shape[1]
    no = width // LANES
    t = S5_T

    def to_chunks(u, length):
        u = u.reshape(bsz, length // t, t, no, LANES).transpose(3, 1, 0, 2, 4)
        u = jnp.pad(u, ((0, 0), (0, 0), (0, SUBLANES - bsz), (0, 0), (0, 0)))
        return u.reshape(no, length // t, SUBLANES, t * LANES)

    n_lat = bsz * seq
    ncc, ncl = n_ctx // t, seq // t
    u_o = jnp.concatenate([to_chunks(u_all[n_lat:], n_ctx), to_chunks(u_all[:n_lat], seq)], axis=1)
    u_o = u_o.reshape(no, (ncc + ncl) * SUBLANES, t * LANES).astype(BF16)
    y_g = s5_scan(_s5_relayout(u_o, split=True), m_mat, e_mat, r_mat, lam, ncc, ncc + ncl)
    y_o = _s5_relayout(y_g, split=False).reshape(no, ncc + ncl, SUBLANES, t, LANES)[:, :, :bsz]

    def from_chunks(y, length):
        return y.transpose(2, 1, 3, 0, 4).reshape(bsz * length, width)

    return jnp.concatenate([from_chunks(y_o[:, ncc:], seq), from_chunks(y_o[:, :ncc], n_ctx)], axis=0)


def _glu_kernel(y_ref, w_ref, b_ref, o_ref):
    gl = _gelu(y_ref[...].astype(F32))
    z = jnp.dot(gl.astype(BF16), w_ref[...], preferred_element_type=F32) + b_ref[...]
    o_ref[...] = (gl * jax.nn.sigmoid(z)).astype(o_ref.dtype)


def s5_glu(y, w, b, tm=512):
    m, n = y.shape
    return pl.pallas_call(
        _glu_kernel,
        grid=(m // tm,),
        in_specs=[pl.BlockSpec((tm, n), lambda i: (i, 0)),
                  pl.BlockSpec((n, n), lambda i: (0, 0)),
                  pl.BlockSpec((1, n), lambda i: (0, 0))],
        out_specs=pl.BlockSpec((tm, n), lambda i: (i, 0)),
        out_shape=jax.ShapeDtypeStruct((m, n), BF16),
        compiler_params=_cparams("arbitrary"),
        name="s5_glu",
    )(y, w, b.reshape(1, n))


_TOMB = 2.0 ** 100
_TOMB_GUARD = 2.0 ** 90


def _top_rows(s, n_take, break_ties):
    vals = []
    if not break_ties:
        for k in range(n_take):
            m = jnp.max(s, axis=0, keepdims=True)
            s = jnp.where(s == m, -_TOMB * (1.0 + k / 32.0), s)
            vals.append(m)
        rank = jnp.where(s <= -_TOMB, (s * (-1.0 / _TOMB) - 1.0) * 32.0, NOT_SELECTED)
        return vals, rank, s
    r = s.shape[0]
    rows = lax.broadcasted_iota(jnp.int32, s.shape, 0).astype(F32)
    rank = jnp.full(s.shape, NOT_SELECTED, F32)
    for k in range(n_take):
        m = jnp.max(s, axis=0, keepdims=True)
        idx = jnp.min(jnp.where(s == m, rows, float(r)), axis=0, keepdims=True)
        sel = rows == idx
        rank = jnp.where(sel, float(k), rank)
        s = jnp.where(sel, -jnp.inf, s)
        vals.append(m)
    return vals, rank, s


_CAND_WIDTHS = [PEER_TOPK // (i + 1) for i in range(PEER_TOPK)]
_CAND_STARTS = [sum(_CAND_WIDTHS[:i]) for i in range(PEER_TOPK + 1)]
_CAND_ROWS = -(-_CAND_STARTS[-1] // SUBLANES) * SUBLANES


def _route_head(s1, s2, grp, break_ties):
    v1, rank1, _ = _top_rows(s1, PEER_TOPK, break_ties)
    v2, rank2, _ = _top_rows(s2, PEER_TOPK, break_ties)
    v2m = jnp.concatenate(v2, axis=0)
    pad = jnp.full((_CAND_ROWS - _CAND_STARTS[-1], v2m.shape[1]), -jnp.inf, F32)
    cand = jnp.concatenate([v1[i] + v2m[0:_CAND_WIDTHS[i]] for i in range(PEER_TOPK)] + [pad], axis=0)
    cv, _, left = _top_rows(cand, PEER_TOPK, break_ties)
    picked = jnp.where(left != cand, 1.0, 0.0)
    picked = jnp.concatenate([picked, jnp.zeros((grp.shape[1] - _CAND_ROWS, picked.shape[1]), F32)], axis=0)
    cnt = jnp.dot(grp, picked, preferred_element_type=F32)
    z = jnp.zeros_like(cv[0])
    for k in range(PEER_TOPK):
        z = z + jnp.exp(cv[k] - cv[0])
    thr = jnp.zeros_like(s1)
    for i in range(PEER_TOPK):
        thr = jnp.where(rank1 == float(i), cnt[i:i + 1, :], thr)
    n_out = (jnp.sum(jnp.where(rank1 != NOT_SELECTED, 1.0, 0.0), axis=0, keepdims=True)
             + jnp.sum(jnp.where(rank2 != NOT_SELECTED, 1.0, 0.0), axis=0, keepdims=True)
             + jnp.sum(cnt, axis=0, keepdims=True))
    return thr, jnp.exp(s1 - v1[0]), rank2.astype(BF16), (jnp.exp(s2 - v2[0]) / z).astype(BF16), n_out


def _peer_route_kernel(q_ref, k1_ref, k2_ref, grp_ref, thr_ref, a_ref, r2_ref, b_ref):
    hp = lax.Precision.HIGHEST
    nt = (((1,), (1,)), ((), ()))
    for h in range(PEER_HEADS):
        qh = q_ref[:, h * 2 * PEER_HALF:(h + 1) * 2 * PEER_HALF]
        s1 = lax.dot_general(k1_ref[h], qh, nt, precision=hp, preferred_element_type=F32)
        s2 = lax.dot_general(k2_ref[h], qh, nt, precision=hp, preferred_element_type=F32)

        def write(res, h=h):
            thr_ref[h], a_ref[h], r2_ref[h], b_ref[h] = res[:4]

        fast = _route_head(s1, s2, grp_ref[...], break_ties=False)
        write(fast)

        low = jnp.minimum(jnp.min(s1), jnp.min(s2)) < -_TOMB_GUARD

        @pl.when((jnp.max(fast[4]) > 3.0 * PEER_TOPK) | low)
        def _(s1=s1, s2=s2, write=write):
            write(_route_head(s1, s2, grp_ref[...], break_ties=True))


def peer_route(q, keys, tt=256):
    t = q.shape[0]
    zeros = jnp.zeros((PEER_HEADS, PEER_NKEYS, PEER_HALF), F32)
    k1 = jnp.concatenate([keys[:, 0], zeros], axis=-1)
    k2 = jnp.concatenate([zeros, keys[:, 1]], axis=-1)
    row = jnp.arange(LANES)[None, :]
    grp = ((row >= jnp.array(_CAND_STARTS[:-1])[:, None]) & (row < jnp.array(_CAND_STARTS[1:])[:, None])).astype(F32)
    shp32 = jax.ShapeDtypeStruct((PEER_HEADS, PEER_NKEYS, t), F32)
    shp16 = jax.ShapeDtypeStruct((PEER_HEADS, PEER_NKEYS, t), BF16)
    ospec = pl.BlockSpec((PEER_HEADS, PEER_NKEYS, tt), lambda i: (0, 0, i))
    kspec = pl.BlockSpec((PEER_HEADS, PEER_NKEYS, 2 * PEER_HALF), lambda i: (0, 0, 0))
    return pl.pallas_call(
        _peer_route_kernel,
        grid=(t // tt,),
        in_specs=[pl.BlockSpec((tt, PEER_HEADS * 2 * PEER_HALF), lambda i: (i, 0)), kspec, kspec,
                  pl.BlockSpec((PEER_TOPK, LANES), lambda i: (0, 0))],
        out_specs=[ospec] * 4,
        out_shape=[shp32, shp32, shp16, shp16],
        compiler_params=_cparams("arbitrary"),
        name="peer_route",
    )(q, k1, k2, grp)


def _rows_bf16(row, n_rows):
    packed = jnp.broadcast_to(row, (2 * SUBLANES, row.shape[1])).astype(BF16)
    return jnp.concatenate([packed] * (n_rows // (2 * SUBLANES)), axis=0)


def _peer_dense_kernel(ft_ref, u_ref, vt_ref, thr_ref, a_ref, r2_ref, b_ref, gate_ref, res_ref, ng_ref, o_ref,
                       acc_scr, ge_scr, *, rows_per_tile, final_norm):
    j = pl.program_id(1)

    @pl.when(j == 0)
    def _():
        acc_scr[...] = jnp.zeros_like(acc_scr)
        ge_scr[...] = jnp.zeros_like(ge_scr)

    w_rows = []
    for r in range(rows_per_tile):
        w = None
        for h in range(PEER_HEADS):
            thr = _rows_bf16(thr_ref[h, r:r + 1, :], PEER_NKEYS)
            a = _rows_bf16(a_ref[h, r:r + 1, :], PEER_NKEYS)
            b = b_ref[h]
            term = jnp.where(r2_ref[h] < thr, b, jnp.zeros_like(b)) * a
            w = term if w is None else w + term
        w_rows.append(w * ge_scr[r * PEER_NKEYS:(r + 1) * PEER_NKEYS, :])
    wt = jnp.concatenate(w_rows, axis=0)
    acc_scr[...] += jnp.dot(vt_ref[...], wt, preferred_element_type=F32)
    act = jnp.dot(u_ref[...], ft_ref[...], preferred_element_type=F32)
    ge_scr[...] = _gelu(act).astype(BF16)

    @pl.when(j == pl.num_programs(1) - 1)
    def _():
        hn = res_ref[...] + gate_ref[...] * acc_scr[...].T
        if final_norm:
            ms = jnp.mean(hn * hn, axis=-1, keepdims=True)
            hn = hn * lax.rsqrt(ms + EPS) * ng_ref[...]
        o_ref[...] = hn


def peer_dense(ft, u, vt, layer, thr, a, r2, b, modr, gate_idx, res, mod_row, norm_g=None, tt=512,
               rows_per_tile=8):
    d, t = ft.shape
    e = u.shape[1]
    te = rows_per_tile * PEER_NKEYS
    ne = e // te

    def cur(j):
        return jnp.minimum(j, ne - 1)

    def prev(j):
        return jnp.maximum(j - 1, 0)

    rspec = pl.BlockSpec((PEER_HEADS, rows_per_tile, tt), lambda i, j: (0, prev(j), i))
    cspec = pl.BlockSpec((PEER_HEADS, PEER_NKEYS, tt), lambda i, j: (0, 0, i))
    return pl.pallas_call(
        functools.partial(_peer_dense_kernel, rows_per_tile=rows_per_tile, final_norm=norm_g is not None),
        grid=(t // tt, ne + 1),
        in_specs=[pl.BlockSpec((d, tt), lambda i, j: (0, i)),
                  pl.BlockSpec((None, te, d), lambda i, j: (layer, cur(j), 0)),
                  pl.BlockSpec((None, d, te), lambda i, j: (layer, 0, prev(j))),
                  rspec, rspec, cspec, cspec,
                  pl.BlockSpec((None, 1, d), lambda i, j: (mod_row(i * tt) * N_MOD + gate_idx, 0, 0)),
                  pl.BlockSpec((tt, d), lambda i, j: (i, 0)),
                  pl.BlockSpec((1, d), lambda i, j: (0, 0))],
        out_specs=pl.BlockSpec((tt, d), lambda i, j: (i, 0)),
        out_shape=jax.ShapeDtypeStruct((t, d), F32),
        scratch_shapes=[pltpu.VMEM((d, tt), F32), pltpu.VMEM((te, tt), BF16)],
        compiler_params=_cparams("arbitrary", "arbitrary"),
        name="peer_dense",
    )(ft, u, vt, thr, a, r2, b, modr, res, (jnp.ones((d,), F32) if norm_g is None else norm_g).reshape(1, d))


def peer_ffn(h, modr, mod_row, norm_g, wq, keys, u, vt, layer, tm, final_norm_g=None):
    q, ft = norm_mod_matmul(h, norm_g, modr, 3, 4, wq.astype(BF16), mod_row, tm=tm, tn=1024, emit_a=True)
    thr, a, r2, b = peer_route(q, keys)
    return peer_dense(ft, u, vt, layer, thr, a, r2, b, modr, 5, h, mod_row, norm_g=final_norm_g)


def _conv_kernel(x_ref, w_ref, b_ref, o_ref, pad_scr, *, length, row_tile):
    half = SSD_CONV // 2
    zeros = jnp.zeros((SUBLANES, x_ref.shape[1]), F32)
    pad_scr[0:SUBLANES, :] = zeros
    pad_scr[SUBLANES + length:2 * SUBLANES + length, :] = zeros
    pad_scr[SUBLANES:SUBLANES + length, :] = x_ref[...].astype(F32)
    for r0 in range(0, length, row_tile):
        acc = jnp.broadcast_to(b_ref[...], (row_tile, x_ref.shape[1]))
        for k in range(SSD_CONV):
            start = SUBLANES + r0 + k - half
            acc = acc + w_ref[k:k + 1, :] * pad_scr[start:start + row_tile, :]
        o_ref[r0:r0 + row_tile, :] = jax.nn.silu(acc).astype(o_ref.dtype)


def ssd_conv(p, col0, width, row0, n_seq, length, conv_w, conv_b, tc=256):
    row_tile = min(length, 512)
    return pl.pallas_call(
        functools.partial(_conv_kernel, length=length, row_tile=row_tile),
        grid=(n_seq, width // tc),
        in_specs=[pl.BlockSpec((length, tc), lambda s, j: (row0 // length + s, col0 // tc + j)),
                  pl.BlockSpec((SSD_CONV, tc), lambda s, j: (0, j)),
                  pl.BlockSpec((1, tc), lambda s, j: (0, j))],
        out_specs=pl.BlockSpec((length, tc), lambda s, j: (s, j)),
        out_shape=jax.ShapeDtypeStruct((n_seq * length, width), BF16),
        scratch_shapes=[pltpu.VMEM((length + 2 * SUBLANES, tc), F32)],
        compiler_params=_cparams("arbitrary", "arbitrary"),
        name="ssd_conv",
    )(p, conv_w, conv_b.reshape(1, width))


def _ssd_kernel(xl_ref, bl_ref, cl_ref, xx_ref, bx_ref, cx_ref, dt_ref, bias_ref, alog_ref, y_ref, st_scr, *,
                reverse, n_ctx_chunks):
    hp = lax.Precision.HIGHEST
    ch = SSD_CHUNK
    nh = dt_ref.shape[1]
    hpg = nh // SSD_GROUPS
    is_ctx = pl.program_id(1) < n_ctx_chunks

    def pick(ctx_ref, lat_ref, cols):
        return jnp.where(is_ctx, ctx_ref[:, cols], lat_ref[:, cols])

    @pl.when(pl.program_id(1) == 0)
    def _():
        st_scr[...] = jnp.zeros_like(st_scr)

    dt = jax.nn.softplus(dt_ref[...] + bias_ref[...])
    la = dt * (-jnp.exp(alog_ref[...]))
    row = lax.broadcasted_iota(jnp.int32, (ch, ch), 0)
    col = lax.broadcasted_iota(jnp.int32, (ch, ch), 1)
    keep = (col >= row) if reverse else (row >= col)
    tri = jnp.where(keep, 1.0, 0.0)
    cs = jnp.dot(tri, la, precision=hp, preferred_element_type=F32)
    both_t = jnp.concatenate([cs, dt], axis=1).T
    cs_t, dt_t = both_t[0:nh], both_t[nh:2 * nh]
    end = 0 if reverse else ch - 1
    tot_t = cs_t[:, end:end + 1]
    ws_t = jnp.exp(tot_t - cs_t) * dt_t
    dec_t = jnp.exp(tot_t)
    ecs = jnp.exp(cs)
    lo = lax.broadcasted_iota(jnp.int32, (ch, 2 * SSD_HEAD_DIM), 1) < SSD_HEAD_DIM
    nt = (((1,), (1,)), ((), ()))
    for g in range(SSD_GROUPS):
        gs = slice(g * SSD_STATE, (g + 1) * SSD_STATE)
        cg = pick(cx_ref, cl_ref, gs)
        bg = pick(bx_ref, bl_ref, gs)
        cb = lax.dot_general(cg, bg, nt, preferred_element_type=F32)
        bg_t = bg.astype(F32).T
        cg32 = cg.astype(F32)
        for pr in range(hpg // 2):
            lanes = slice((g * hpg + 2 * pr) * SSD_HEAD_DIM, (g * hpg + 2 * pr + 2) * SSD_HEAD_DIM)
            sl = slice(2 * pr * SSD_HEAD_DIM, (2 * pr + 2) * SSD_HEAD_DIM)
            xp = pick(xx_ref, xl_ref, lanes)
            st = st_scr[g, :, sl]
            rhs = jnp.concatenate([xp, st.astype(BF16)], axis=0)
            ys, news, decs = [], [], []
            for j in range(2):
                h = g * hpg + 2 * pr + j
                seg = cs[:, h:h + 1] - cs_t[h:h + 1, :]
                gmat = jnp.where(keep, jnp.exp(seg), 0.0) * cb * dt_t[h:h + 1, :]
                cmat = cg32 * ecs[:, h:h + 1]
                lhs = jnp.concatenate([gmat, cmat], axis=1).astype(BF16)
                ys.append(jnp.dot(lhs, rhs, preferred_element_type=F32))
                bw = (bg_t * ws_t[h:h + 1, :]).astype(BF16)
                news.append(jnp.dot(bw, xp, preferred_element_type=F32))
                decs.append(jnp.broadcast_to(dec_t[h:h + 1, :], (ch, 2 * SSD_HEAD_DIM)))
            y_ref[:, lanes] = jnp.where(lo, ys[0], ys[1]).astype(y_ref.dtype)
            st_scr[g, :, sl] = st * jnp.where(lo, decs[0], decs[1]) + jnp.where(lo, news[0], news[1])


def ssd_scan(xc_lat, xc_ctx, dt_raw, dt_bias, a_log, bsz, seq, n_ctx, reverse):
    nh = dt_raw.shape[1]
    inner = nh * SSD_HEAD_DIM
    gn = SSD_GROUPS * SSD_STATE
    nl, nc = seq // SSD_CHUNK, n_ctx // SSD_CHUNK
    lat_blocks = bsz * nl

    def lblk(b, s):
        sl = jnp.maximum(s - nc, 0)
        return b * nl + ((nl - 1 - sl) if reverse else sl)

    def cblk(b, s):
        sc = jnp.minimum(s, nc - 1)
        return b * nc + ((nc - 1 - sc) if reverse else sc)

    def blk(b, s):
        return jnp.where(s < nc, lat_blocks + cblk(b, s), lblk(b, s))

    def xbc_specs(fn):
        return [pl.BlockSpec((SSD_CHUNK, inner), lambda b, s: (fn(b, s), 0)),
                pl.BlockSpec((SSD_CHUNK, gn), lambda b, s: (fn(b, s), inner // gn)),
                pl.BlockSpec((SSD_CHUNK, gn), lambda b, s: (fn(b, s), inner // gn + 1))]

    return pl.pallas_call(
        functools.partial(_ssd_kernel, reverse=reverse, n_ctx_chunks=nc),
        grid=(bsz, nc + nl),
        in_specs=xbc_specs(lblk) + xbc_specs(cblk) + [
            pl.BlockSpec((SSD_CHUNK, nh), lambda b, s: (blk(b, s), 0)),
            pl.BlockSpec((1, nh), lambda b, s: (0, 0)),
            pl.BlockSpec((1, nh), lambda b, s: (0, 0))],
        out_specs=pl.BlockSpec((SSD_CHUNK, inner), lambda b, s: (lblk(b, s), 0)),
        out_shape=jax.ShapeDtypeStruct((bsz * seq, inner), BF16),
        scratch_shapes=[pltpu.VMEM((SSD_GROUPS, SSD_STATE, inner // SSD_GROUPS), F32)],
        compiler_params=_cparams("arbitrary", "arbitrary"),
        name="ssd_scan_bwd" if reverse else "ssd_scan_fwd",
    )(xc_lat, xc_lat, xc_lat, xc_ctx, xc_ctx, xc_ctx, dt_raw, dt_bias.reshape(1, nh), a_log.reshape(1, nh))


def _ssd_out_kernel(yf_ref, yb_ref, x_ref, z_ref, d_ref, ng_ref, w_ref, gate_ref, res_ref, o_ref, a_scr):
    @pl.when(pl.program_id(1) == 0)
    def _():
        y = (x_ref[...].astype(F32) * d_ref[...] + yf_ref[...].astype(F32) + yb_ref[...].astype(F32))
        gv = y * jax.nn.silu(z_ref[...].astype(F32))
        gw = gv.shape[1] // SSD_GROUPS
        for g in range(SSD_GROUPS):
            part = gv[:, g * gw:(g + 1) * gw]
            ms = jnp.mean(part * part, axis=-1, keepdims=True)
            a_scr[:, g * gw:(g + 1) * gw] = (part * lax.rsqrt(ms + EPS) * ng_ref[:, g * gw:(g + 1) * gw]).astype(BF16)

    acc = jnp.dot(a_scr[...], w_ref[...], preferred_element_type=F32)
    o_ref[...] = res_ref[...] + gate_ref[...] * acc


def ssd_output(yf, yb, xc, pz, d_exp, norm_g, w_out, modr, res, mod_row, tm=512, tn=512):
    m, inner = yf.shape
    n = w_out.shape[1]
    return pl.pallas_call(
        _ssd_out_kernel,
        grid=(m // tm, n // tn),
        in_specs=[pl.BlockSpec((tm, inner), lambda i, j: (i, 0)),
                  pl.BlockSpec((tm, inner), lambda i, j: (i, 0)),
                  pl.BlockSpec((tm, inner), lambda i, j: (i, 0)),
                  pl.BlockSpec((tm, inner), lambda i, j: (i, 0)),
                  pl.BlockSpec((1, inner), lambda i, j: (0, 0)),
                  pl.BlockSpec((1, inner), lambda i, j: (0, 0)),
                  pl.BlockSpec((inner, tn), lambda i, j: (0, j)),
                  pl.BlockSpec((None, 1, tn), lambda i, j: (mod_row(i * tm) * N_MOD + 2, 0, j)),
                  pl.BlockSpec((tm, tn), lambda i, j: (i, j))],
        out_specs=pl.BlockSpec((tm, tn), lambda i, j: (i, j)),
        out_shape=jax.ShapeDtypeStruct((m, n), F32),
        scratch_shapes=[pltpu.VMEM((tm, inner), BF16)],
        compiler_params=_cparams("arbitrary", "arbitrary"),
        name="ssd_output",
    )(yf, yb, xc, pz, d_exp.reshape(1, inner), norm_g.reshape(1, inner), w_out, modr, res)


def _mod_rows(c, c_ctx, w, b, layer):
    bsz, d = c.shape
    cond = jnp.zeros((SUBLANES, d), F32).at[:bsz].set(c).at[bsz].set(c_ctx)
    mod = modulation(cond, w, b, layer)
    return mod.reshape(SUBLANES * N_MOD, 1, d)


def attn_s5_layer(h, bsz, seq, n_ctx, modr, mod_row, tm, norm1_g, w_in, sink, s5_params, glu_w, glu_b, w_out):
    n_lat = bsz * seq
    qw = ATTN_HEADS * HEAD_DIM
    kvw = ATTN_KV_HEADS * HEAD_DIM
    p = norm_mod_matmul(h, norm1_g, modr, 0, 1, w_in.astype(BF16), mod_row, tm=tm, tn=1280, out_dtype=BF16)
    attn_lat = attention_latent(p, sink, bsz, seq, n_ctx)
    attn_ctx = attention_context(p, sink, bsz, seq, n_ctx)
    attn = jnp.concatenate([attn_lat, attn_ctx], axis=0)
    y = s5_layer(p[:, qw + 2 * kvw:], bsz, seq, n_ctx, s5_params)
    s5o = s5_glu(y, glu_w.astype(BF16), glu_b)
    w_out = w_out.astype(BF16)
    return matmul_gated_residual([attn, s5o], [w_out[:qw], w_out[qw:]], modr, 2, h, mod_row, tm=tm, tn=1024)


def ssd_layer(h, bsz, seq, n_ctx, modr, mod_row, tm, norm1_g, w_in, conv_w, conv_b, dt_bias, a_log, d_skip,
              norm_g, w_out):
    n_lat = bsz * seq
    nh = a_log.shape[1]
    inner = nh * SSD_HEAD_DIM
    xbc_w = inner + 2 * SSD_GROUPS * SSD_STATE
    w_in = w_in.astype(BF16)
    pz = norm_mod_matmul(h, norm1_g, modr, 0, 1, w_in[:, :inner + xbc_w], mod_row, tm=tm, tn=2048,
                         out_dtype=BF16)
    dt_raw = norm_mod_matmul(h, norm1_g, modr, 0, 1, w_in[:, inner + xbc_w:], mod_row, tm=tm, tn=2 * nh)
    xc = ssd_conv(pz, inner, xbc_w, 0, bsz, seq, conv_w, conv_b)
    xc_ctx = ssd_conv(pz, inner, xbc_w, n_lat, bsz, n_ctx, conv_w, conv_b)
    yf = ssd_scan(xc, xc_ctx, dt_raw[:, :nh], dt_bias[0], a_log[0], bsz, seq, n_ctx, reverse=False)
    yb = ssd_scan(xc, xc_ctx, dt_raw[:, nh:], dt_bias[1], a_log[1], bsz, seq, n_ctx, reverse=True)
    d_exp = jnp.repeat(d_skip, SSD_HEAD_DIM)
    return ssd_output(yf, yb, xc, pz, d_exp, norm_g, w_out.astype(BF16), modr, h, mod_row)


def kernel(x, c, ctx, c_ctx, mod_w, mod_b, norm1_g, norm2_g, ab_w_in, attn_sink, s5_a_re, s5_a_im, s5_log_dt, s5_b_re, s5_b_im, s5_c_re, s5_c_im, s5_d, s5_glu_w, s5_glu_b, ab_w_out, ssd_w_in, ssd_conv_w, ssd_conv_b, ssd_dt_bias, ssd_a_log, ssd_d, ssd_norm_g, ssd_w_out, peer_wq, peer_keys, peer_u, peer_v, final_norm_g):
    bsz, seq, d = x.shape
    n_ctx = ctx.shape[1]
    n_lat = bsz * seq
    depth = mod_w.shape[0]
    tm = 512

    def mod_row(r):
        return jnp.minimum(r // seq, bsz)

    h = jnp.concatenate([x.reshape(n_lat, d), ctx.reshape(bsz * n_ctx, d)], axis=0)
    peer_u16 = peer_u.astype(BF16)
    peer_vt16 = peer_v.transpose(0, 2, 1).astype(BF16)
    for i in range(depth):
        last = i == depth - 1
        j = i // 2
        modr = _mod_rows(c, c_ctx, mod_w, mod_b, i)
        if i % 2 == 0:
            s5_params = (s5_a_re[j], s5_a_im[j], s5_log_dt[j], s5_b_re[j], s5_b_im[j], s5_c_re[j], s5_c_im[j],
                         s5_d[j])
            h = attn_s5_layer(h, bsz, seq, n_ctx, modr, mod_row, tm, norm1_g[i], ab_w_in[j], attn_sink[j],
                              s5_params, s5_glu_w[j], s5_glu_b[j], ab_w_out[j])
        else:
            if not last:
                raise NotImplementedError("an SSD layer that must also update the context stream")
            h = ssd_layer(h, bsz, seq, n_ctx, modr, mod_row, tm, norm1_g[i], ssd_w_in[j], ssd_conv_w[j],
                          ssd_conv_b[j], ssd_dt_bias[j], ssd_a_log[j], ssd_d[j], ssd_norm_g[j], ssd_w_out[j])
        h = peer_ffn(h, modr, mod_row, norm2_g[i], peer_wq[i], peer_keys[i], peer_u16, peer_vt16, i, tm,
                     final_norm_g=final_norm_g if last else None)
    return h[:n_lat].reshape(bsz, seq, d)
```

```python
import functools
import math

import jax
import jax.numpy as jnp
from jax import lax
from jax.experimental import pallas as pl
from jax.experimental.pallas import tpu as pltpu

F32 = jnp.float32
BF16 = jnp.bfloat16
EPS = 1e-6

V7X_VMEM_LIMIT_BYTES = 56 * 1024 * 1024
LANES = 128
SUBLANES = 8

N_MOD = 6
HEAD_DIM = 128
ATTN_HEADS = 8
ATTN_KV_HEADS = 2
ATTN_GROUP = ATTN_HEADS // ATTN_KV_HEADS
ATTN_SCALE = HEAD_DIM ** -0.5
BLOCK = 128
GRID_W = 64
ROPE_THETA = 10000.0
S5_CH = 16
S5_STATE = 64
S5_T = 16
S5_GB = 2
SSD_HEAD_DIM = 64
SSD_GROUPS = 8
SSD_STATE = 128
SSD_CONV = 5
SSD_CHUNK = 128
PEER_HEADS = 8
PEER_NKEYS = 128
PEER_HALF = 64
PEER_TOPK = 16
NOT_SELECTED = 127.0


def _cparams(*sem):
    return pltpu.CompilerParams(dimension_semantics=sem, vmem_limit_bytes=V7X_VMEM_LIMIT_BYTES)


def _gelu(x):
    return 0.5 * x * (1.0 + lax.erf(x * (2.0 ** -0.5)))


def _mod_kernel(c_ref, w_ref, b_ref, o_ref):
    s = jax.nn.silu(c_ref[...]).astype(BF16)
    o_ref[...] = jnp.dot(s, w_ref[...].astype(BF16), preferred_element_type=F32) + b_ref[...]


def modulation(cond, w, b, layer, tn=1024):
    rows, d = cond.shape
    n = w.shape[2]
    return pl.pallas_call(
        _mod_kernel,
        grid=(n // tn,),
        in_specs=[pl.BlockSpec((rows, d), lambda j: (0, 0)),
                  pl.BlockSpec((None, d, tn), lambda j: (layer, 0, j)),
                  pl.BlockSpec((None, 1, tn), lambda j: (layer, 0, j))],
        out_specs=pl.BlockSpec((rows, tn), lambda j: (0, j)),
        out_shape=jax.ShapeDtypeStruct((rows, n), F32),
        compiler_params=_cparams("arbitrary"),
        name="modulation",
    )(cond, w, b.reshape(b.shape[0], 1, n))


def _nmm_kernel(x_ref, g_ref, sh_ref, sc_ref, w_ref, *refs, emit_a):
    if emit_a:
        o_ref, a_out_ref, a_scr = refs
    else:
        o_ref, a_scr = refs

    @pl.when(pl.program_id(1) == 0)
    def _():
        x = x_ref[...]
        ms = jnp.mean(x * x, axis=-1, keepdims=True)
        y = x * lax.rsqrt(ms + EPS) * g_ref[...]
        a = y * (1.0 + sc_ref[...]) + sh_ref[...]
        a_scr[...] = a.astype(BF16)
        if emit_a:
            a_out_ref[...] = a.T.astype(BF16)

    o_ref[...] = jnp.dot(a_scr[...], w_ref[...], preferred_element_type=F32).astype(o_ref.dtype)


def norm_mod_matmul(x, g, modr, shift_idx, scale_idx, w, mod_row, tm, tn, out_dtype=F32, emit_a=False):
    m, d = x.shape
    n = w.shape[1]
    out_shape = [jax.ShapeDtypeStruct((m, n), out_dtype)]
    out_specs = [pl.BlockSpec((tm, tn), lambda i, j: (i, j))]
    if emit_a:
        out_shape.append(jax.ShapeDtypeStruct((d, m), BF16))
        out_specs.append(pl.BlockSpec((d, tm), lambda i, j: (0, i)))
    res = pl.pallas_call(
        functools.partial(_nmm_kernel, emit_a=emit_a),
        grid=(m // tm, n // tn),
        in_specs=[pl.BlockSpec((tm, d), lambda i, j: (i, 0)),
                  pl.BlockSpec((1, d), lambda i, j: (0, 0)),
                  pl.BlockSpec((None, 1, d), lambda i, j: (mod_row(i * tm) * N_MOD + shift_idx, 0, 0)),
                  pl.BlockSpec((None, 1, d), lambda i, j: (mod_row(i * tm) * N_MOD + scale_idx, 0, 0)),
                  pl.BlockSpec((d, tn), lambda i, j: (0, j))],
        out_specs=out_specs,
        out_shape=out_shape,
        scratch_shapes=[pltpu.VMEM((tm, d), BF16)],
        compiler_params=_cparams("arbitrary", "arbitrary"),
        name="norm_mod_matmul",
    )(x, g.reshape(1, d), modr, modr, w)
    return res if emit_a else res[0]


def _mmres_kernel(*refs, n_parts):
    xs = refs[:n_parts]
    ws = refs[n_parts:2 * n_parts]
    gate_ref, res_ref, o_ref = refs[2 * n_parts:]
    acc = jnp.dot(xs[0][...], ws[0][...], preferred_element_type=F32)
    for p in range(1, n_parts):
        acc = acc + jnp.dot(xs[p][...], ws[p][...], preferred_element_type=F32)
    o_ref[...] = res_ref[...] + gate_ref[...] * acc


def matmul_gated_residual(xs, ws, modr, gate_idx, res, mod_row, tm, tn):
    m, n = xs[0].shape[0], res.shape[1]
    n_parts = len(xs)
    in_specs = ([pl.BlockSpec((tm, x.shape[1]), lambda i, j: (i, 0)) for x in xs]
                + [pl.BlockSpec((w.shape[0], tn), lambda i, j: (0, j)) for w in ws]
                + [pl.BlockSpec((None, 1, tn), lambda i, j: (mod_row(i * tm) * N_MOD + gate_idx, 0, j)),
                   pl.BlockSpec((tm, tn), lambda i, j: (i, j))])
    return pl.pallas_call(
        functools.partial(_mmres_kernel, n_parts=n_parts),
        grid=(m // tm, n // tn),
        in_specs=in_specs,
        out_specs=pl.BlockSpec((tm, tn), lambda i, j: (i, j)),
        out_shape=jax.ShapeDtypeStruct((m, n), F32),
        compiler_params=_cparams("arbitrary", "arbitrary"),
        name="matmul_gated_residual",
    )(*xs, *ws, modr, res)


def rope_tables(seq):
    t = jnp.arange(seq)
    row = (t // GRID_W).astype(F32)
    col = (t % GRID_W).astype(F32)
    n_freq = HEAD_DIM // 4
    inv = ROPE_THETA ** (-jnp.arange(n_freq, dtype=F32) / n_freq)
    ang = jnp.concatenate([row[:, None] * inv, col[:, None] * inv], axis=-1)
    cos, sin = jnp.cos(ang), jnp.sin(ang)
    return jnp.concatenate([cos, cos], axis=-1), jnp.concatenate([-sin, sin], axis=-1)


def _attn_group(q, k, v, mask, sink_col):
    s = lax.dot_general(q, k, (((1,), (1,)), ((), ())), preferred_element_type=F32) * ATTN_SCALE
    if mask is not None:
        s = jnp.where(mask, s, -jnp.inf)
    m = jnp.maximum(jnp.max(s, axis=-1, keepdims=True), sink_col)
    e = jnp.exp(s - m)
    den = jnp.sum(e, axis=-1, keepdims=True) + jnp.exp(sink_col - m)
    o = jnp.dot(e.astype(BF16), v, preferred_element_type=F32)
    return o / den


def _sink_col(sink_ref, hk):
    return jnp.concatenate([jnp.full((BLOCK, 1), sink_ref[hk * ATTN_GROUP + g], F32)
                            for g in range(ATTN_GROUP)], axis=0)


def _rope(x, cos_ref, sin_ref):
    x = x.astype(F32)
    return (x * cos_ref[...] + pltpu.roll(x, HEAD_DIM // 2, 1) * sin_ref[...]).astype(BF16)


def _attn_lat_kernel(sink_ref, q_ref, kp_ref, kc_ref, kn_ref, vp_ref, vc_ref, vn_ref, kx_ref, vx_ref,
                     cp_ref, cc_ref, cn_ref, sp_ref, sc_ref, sn_ref, o_ref, *, n_blocks, n_ctx):
    n = pl.program_id(1)
    row = lax.broadcasted_iota(jnp.int32, (BLOCK, BLOCK), 0)
    col = lax.broadcasted_iota(jnp.int32, (BLOCK, BLOCK), 1)
    m_prev = (col >= row) & (n > 0)
    m_next = (col <= row) & (n < n_blocks - 1)
    ones_ctx = jnp.ones((BLOCK, n_ctx), jnp.bool_)
    ones_cur = jnp.ones((BLOCK, BLOCK), jnp.bool_)
    mask1 = jnp.concatenate([ones_ctx, m_prev, ones_cur, m_next], axis=1)
    mask = jnp.concatenate([mask1] * ATTN_GROUP, axis=0)
    for hk in range(ATTN_KV_HEADS):
        hs = slice(hk * HEAD_DIM, (hk + 1) * HEAD_DIM)
        k = jnp.concatenate([kx_ref[:, hs].astype(BF16), _rope(kp_ref[:, hs], cp_ref, sp_ref),
                             _rope(kc_ref[:, hs], cc_ref, sc_ref), _rope(kn_ref[:, hs], cn_ref, sn_ref)], axis=0)
        v = jnp.concatenate([vx_ref[:, hs], vp_ref[:, hs], vc_ref[:, hs], vn_ref[:, hs]], axis=0).astype(BF16)
        q = jnp.concatenate([_rope(q_ref[:, (hk * ATTN_GROUP + g) * HEAD_DIM:(hk * ATTN_GROUP + g + 1) * HEAD_DIM],
                                   cc_ref, sc_ref) for g in range(ATTN_GROUP)], axis=0)
        o = _attn_group(q, k, v, mask, _sink_col(sink_ref, hk))
        for g in range(ATTN_GROUP):
            h = hk * ATTN_GROUP + g
            o_ref[:, h * HEAD_DIM:(h + 1) * HEAD_DIM] = o[g * BLOCK:(g + 1) * BLOCK].astype(o_ref.dtype)


def attention_latent(p, sink, bsz, seq, n_ctx):
    cos, sin = rope_tables(seq)
    nb = seq // BLOCK
    kvw = ATTN_KV_HEADS * HEAD_DIM
    qw = ATTN_HEADS * HEAD_DIM
    kcol, vcol = qw // kvw, (qw + kvw) // kvw
    ctx_blk0 = bsz * seq // n_ctx

    def prev(b, n):
        return b * nb + jnp.maximum(n - 1, 0)

    def nxt(b, n):
        return b * nb + jnp.minimum(n + 1, nb - 1)

    return pl.pallas_call(
        functools.partial(_attn_lat_kernel, n_blocks=nb, n_ctx=n_ctx),
        grid=(bsz, nb),
        in_specs=[pl.BlockSpec(memory_space=pltpu.SMEM),
                  pl.BlockSpec((BLOCK, qw), lambda b, n: (b * nb + n, 0)),
                  pl.BlockSpec((BLOCK, kvw), lambda b, n: (prev(b, n), kcol)),
                  pl.BlockSpec((BLOCK, kvw), lambda b, n: (b * nb + n, kcol)),
                  pl.BlockSpec((BLOCK, kvw), lambda b, n: (nxt(b, n), kcol)),
                  pl.BlockSpec((BLOCK, kvw), lambda b, n: (prev(b, n), vcol)),
                  pl.BlockSpec((BLOCK, kvw), lambda b, n: (b * nb + n, vcol)),
                  pl.BlockSpec((BLOCK, kvw), lambda b, n: (nxt(b, n), vcol)),
                  pl.BlockSpec((n_ctx, kvw), lambda b, n: (ctx_blk0 + b, kcol)),
                  pl.BlockSpec((n_ctx, kvw), lambda b, n: (ctx_blk0 + b, vcol))]
        + [pl.BlockSpec((BLOCK, HEAD_DIM), fn) for fn in (
            lambda b, n: (jnp.maximum(n - 1, 0), 0), lambda b, n: (n, 0),
            lambda b, n: (jnp.minimum(n + 1, nb - 1), 0))] * 2,
        out_specs=pl.BlockSpec((BLOCK, qw), lambda b, n: (b * nb + n, 0)),
        out_shape=jax.ShapeDtypeStruct((bsz * seq, qw), BF16),
        compiler_params=_cparams("arbitrary", "arbitrary"),
        name="attention_latent",
    )(sink, p, p, p, p, p, p, p, p, p, cos, cos, cos, sin, sin, sin)


def _attn_ctx_kernel(sink_ref, q_ref, k_ref, v_ref, o_ref, *, n_ctx):
    for hk in range(ATTN_KV_HEADS):
        hs = slice(hk * HEAD_DIM, (hk + 1) * HEAD_DIM)
        k = k_ref[:, hs].astype(BF16)
        v = v_ref[:, hs].astype(BF16)
        for g in range(ATTN_GROUP):
            h = hk * ATTN_GROUP + g
            q = q_ref[:, h * HEAD_DIM:(h + 1) * HEAD_DIM].astype(BF16)
            sink_col = jnp.full((n_ctx, 1), sink_ref[h], F32)
            o = _attn_group(q, k, v, None, sink_col)
            o_ref[:, h * HEAD_DIM:(h + 1) * HEAD_DIM] = o.astype(o_ref.dtype)


def attention_context(p, sink, bsz, seq, n_ctx):
    kvw = ATTN_KV_HEADS * HEAD_DIM
    qw = ATTN_HEADS * HEAD_DIM
    kcol, vcol = qw // kvw, (qw + kvw) // kvw
    ctx_blk0 = bsz * seq // n_ctx
    return pl.pallas_call(
        functools.partial(_attn_ctx_kernel, n_ctx=n_ctx),
        grid=(bsz,),
        in_specs=[pl.BlockSpec(memory_space=pltpu.SMEM),
                  pl.BlockSpec((n_ctx, qw), lambda b: (ctx_blk0 + b, 0)),
                  pl.BlockSpec((n_ctx, kvw), lambda b: (ctx_blk0 + b, kcol)),
                  pl.BlockSpec((n_ctx, kvw), lambda b: (ctx_blk0 + b, vcol))],
        out_specs=pl.BlockSpec((n_ctx, qw), lambda b: (b, 0)),
        out_shape=jax.ShapeDtypeStruct((bsz * n_ctx, qw), BF16),
        compiler_params=_cparams("arbitrary"),
        name="attention_context",
    )(sink, p, p, p)


def _cmul(ar, ai, br, bi):
    return ar * br - ai * bi, ar * bi + ai * br


def s5_group_matrices(a_re, a_im, log_dt, b_re, b_im, c_re, c_im, d):
    hp = lax.Precision.HIGHEST
    t = S5_T
    dt = jnp.exp(log_dt)
    mag = jnp.exp(dt * a_re)
    lr, li = mag * jnp.cos(dt * a_im), mag * jnp.sin(dt * a_im)
    den = a_re * a_re + a_im * a_im
    f_re = ((lr - 1) * a_re + li * a_im) / den
    f_im = (li * a_re - (lr - 1) * a_im) / den
    bb_re = f_re[..., None] * b_re - f_im[..., None] * b_im
    bb_im = f_re[..., None] * b_im + f_im[..., None] * b_re
    pr, pi = [jnp.ones_like(lr)], [jnp.zeros_like(li)]
    for _ in range(t):
        nr, ni = _cmul(pr[-1], pi[-1], lr, li)
        pr.append(nr)
        pi.append(ni)
    pw_re, pw_im = jnp.stack(pr, axis=1), jnp.stack(pi, axis=1)
    cl_re = c_re[:, None] * pw_re[:, :, :, None, :] - c_im[:, None] * pw_im[:, :, :, None, :]
    cl_im = c_re[:, None] * pw_im[:, :, :, None, :] + c_im[:, None] * pw_re[:, :, :, None, :]
    kern = jnp.einsum('dkgcp,dgpe->dkgce', jnp.concatenate([cl_re, -cl_im], axis=-1),
                      jnp.concatenate([bb_re, bb_im], axis=2), precision=hp)
    g = a_re.shape[1]
    ii = jnp.arange(t)[:, None]
    jj = jnp.arange(t)[None, :]
    kf = kern[0][jnp.clip(jj - ii, 0, t)]
    kb = kern[1][jnp.clip(ii - jj, 0, t)]
    mf = (ii <= jj)[:, :, None, None, None]
    mb = (ii >= jj)[:, :, None, None, None]
    eye = (ii == jj)[:, :, None, None, None] * jnp.eye(S5_CH, dtype=F32)[None, None, None] * d[None, None, :, :, None]
    mm = jnp.where(mf, kf, 0.0) + jnp.where(mb, kb, 0.0) + eye
    m_mat = mm.transpose(2, 0, 4, 1, 3).reshape(g, t * S5_CH, t * S5_CH)
    pf = jnp.arange(t - 1, -1, -1)
    pb = jnp.arange(t)

    def drive(dirn, pidx):
        er, ei = _cmul(pw_re[dirn][pidx][:, :, :, None], pw_im[dirn][pidx][:, :, :, None],
                       bb_re[dirn][None], bb_im[dirn][None])
        er = er.transpose(1, 0, 3, 2).reshape(g, t * S5_CH, S5_STATE)
        ei = ei.transpose(1, 0, 3, 2).reshape(g, t * S5_CH, S5_STATE)
        return er, ei

    ef_re, ef_im = drive(0, pf)
    eb_re, eb_im = drive(1, pb)
    e_mat = jnp.concatenate([ef_re, eb_re, ef_im, eb_im], axis=-1)
    rf = jnp.arange(1, t + 1)
    rb = jnp.arange(t, 0, -1)

    def read(dirn, pidx):
        rr = cl_re[dirn][pidx]
        ri = cl_im[dirn][pidx]
        rr = rr.transpose(1, 3, 0, 2).reshape(g, S5_STATE, t * S5_CH)
        ri = ri.transpose(1, 3, 0, 2).reshape(g, S5_STATE, t * S5_CH)
        return rr, -ri

    rf_re, rf_im = read(0, rf)
    rb_re, rb_im = read(1, rb)
    r_mat = jnp.concatenate([rf_re, rb_re, rf_im, rb_im], axis=1)
    lam = jnp.stack([jnp.concatenate([pw_re[0, t], pw_re[1, t]], axis=-1),
                     jnp.concatenate([pw_im[0, t], pw_im[1, t]], axis=-1)], axis=1)
    gb, nb, p, w = S5_GB, g // S5_GB, S5_STATE, t * S5_CH
    grp = jnp.arange(gb)[:, None, None]
    src = jnp.arange(w)[None, :, None]
    dst = jnp.arange(gb * w)[None, None, :]
    place_tc = (dst == (src // S5_CH) * (gb * S5_CH) + grp * S5_CH + src % S5_CH).astype(F32)
    src = jnp.arange(4 * p)[None, :, None]
    dst = jnp.arange(4 * gb * p)[None, None, :]
    place_st = (dst == (src // p) * (gb * p) + grp * p + src % p).astype(F32)
    pack = lambda left, mat, right: jnp.einsum('gri,ogrs,gsj->oij', left, mat.reshape((nb, gb) + mat.shape[1:]),
                                               right).astype(BF16)
    lam_blk = lam.reshape(nb, gb, 2, 2, p).transpose(0, 2, 3, 1, 4).reshape(nb, 2, 2 * gb * p)
    return (pack(place_tc, m_mat, place_tc), pack(place_tc, e_mat, place_st), pack(place_st, r_mat, place_tc),
            lam_blk)


def _s5_kernel(u_ref, m_ref, e_ref, r_ref, lam_ref, y_ref, d_scr, sp_scr, *, n_ctx_chunks, n_chunks):
    u = u_ref[...]
    d_scr[...] = jnp.dot(u, e_ref[...], preferred_element_type=F32)
    sl = S5_GB * S5_STATE
    lam_re = jnp.broadcast_to(lam_ref[0:1, :], (SUBLANES, 2 * sl))
    lam_im = jnp.broadcast_to(lam_ref[1:2, :], (SUBLANES, 2 * sl))

    def rows(n):
        return pl.ds(pl.multiple_of(n * SUBLANES, SUBLANES), SUBLANES)

    def step(s, carry):
        s_re, s_im = carry
        nf = s
        nb = jnp.where(s < n_ctx_chunks, n_ctx_chunks - 1 - s, n_chunks - 1 - (s - n_ctx_chunks))
        sp_scr[rows(nf), 0:sl] = s_re[:, 0:sl]
        sp_scr[rows(nf), 2 * sl:3 * sl] = s_im[:, 0:sl]
        sp_scr[rows(nb), sl:2 * sl] = s_re[:, sl:2 * sl]
        sp_scr[rows(nb), 3 * sl:4 * sl] = s_im[:, sl:2 * sl]
        d_re = jnp.concatenate([d_scr[rows(nf), 0:sl], d_scr[rows(nb), sl:2 * sl]], axis=1)
        d_im = jnp.concatenate([d_scr[rows(nf), 2 * sl:3 * sl], d_scr[rows(nb), 3 * sl:4 * sl]], axis=1)
        n_re = lam_re * s_re - lam_im * s_im + d_re
        n_im = lam_re * s_im + lam_im * s_re + d_im
        return n_re, n_im

    zero = jnp.zeros((SUBLANES, 2 * sl), F32)
    lax.fori_loop(0, n_chunks, step, (zero, zero))
    y_ref[...] = (jnp.dot(u, m_ref[...], preferred_element_type=F32)
                  + jnp.dot(sp_scr[...].astype(BF16), r_ref[...], preferred_element_type=F32)).astype(y_ref.dtype)


def s5_scan(u_g, m_mat, e_mat, r_mat, lam, n_ctx_chunks, n_chunks):
    nb, r, w = u_g.shape
    sw = e_mat.shape[2]
    return pl.pallas_call(
        functools.partial(_s5_kernel, n_ctx_chunks=n_ctx_chunks, n_chunks=n_chunks),
        grid=(nb,),
        in_specs=[pl.BlockSpec((None, r, w), lambda i: (i, 0, 0)),
                  pl.BlockSpec((None, w, w), lambda i: (i, 0, 0)),
                  pl.BlockSpec((None, w, sw), lambda i: (i, 0, 0)),
                  pl.BlockSpec((None, sw, w), lambda i: (i, 0, 0)),
                  pl.BlockSpec((None, 2, sw // 2), lambda i: (i, 0, 0))],
        out_specs=pl.BlockSpec((None, r, w), lambda i: (i, 0, 0)),
        out_shape=jax.ShapeDtypeStruct((nb, r, w), BF16),
        scratch_shapes=[pltpu.VMEM((r, sw), F32), pltpu.VMEM((r, sw), F32)],
        compiler_params=_cparams("arbitrary"),
        name="s5_scan",
    )(u_g, m_mat, e_mat, r_mat, lam)


def _s5_split_kernel(x_ref, o_ref):
    x = x_ref[...]
    bw = S5_GB * S5_CH
    for q in range(LANES // bw):
        o_ref[q] = jnp.concatenate([x[:, j * LANES + q * bw:j * LANES + (q + 1) * bw] for j in range(S5_T)], axis=1)


def _s5_merge_kernel(y_ref, o_ref):
    bw = S5_GB * S5_CH
    o_ref[...] = jnp.concatenate([y_ref[q][:, j * bw:(j + 1) * bw]
                                  for j in range(S5_T) for q in range(LANES // bw)], axis=1)


def _s5_relayout(x, split):
    k = LANES // (S5_GB * S5_CH)
    wide, narrow = S5_T * LANES, S5_T * S5_GB * S5_CH
    r = x.shape[1]
    row_tile = r // 4 if r % (4 * 2 * SUBLANES) == 0 else r
    no = x.shape[0] if split else x.shape[0] // k
    wide_spec = pl.BlockSpec((None, row_tile, wide), lambda o, i: (o, i, 0))
    narrow_spec = pl.BlockSpec((k, row_tile, narrow), lambda o, i: (o, i, 0))
    return pl.pallas_call(
        _s5_split_kernel if split else _s5_merge_kernel,
        grid=(no, r // row_tile),
        in_specs=[wide_spec if split else narrow_spec],
        out_specs=narrow_spec if split else wide_spec,
        out_shape=jax.ShapeDtypeStruct((no * k, r, narrow) if split else (no, r, wide), x.dtype),
        compiler_params=_cparams("arbitrary", "arbitrary"),
        name="s5_split" if split else "s5_merge",
    )(x)


def s5_layer(u_all, bsz, seq, n_ctx, params):
    m_mat, e_mat, r_mat, lam = s5_group_matrices(*params)
    width = u_all.shape[1]
    no = width // LANES
    t = S5_T

    def to_chunks(u, length):
        u = u.reshape(bsz, length // t, t, no, LANES).transpose(3, 1, 0, 2, 4)
        u = jnp.pad(u, ((0, 0), (0, 0), (0, SUBLANES - bsz), (0, 0), (0, 0)))
        return u.reshape(no, length // t, SUBLANES, t * LANES)

    n_lat = bsz * seq
    ncc, ncl = n_ctx // t, seq // t
    u_o = jnp.concatenate([to_chunks(u_all[n_lat:], n_ctx), to_chunks(u_all[:n_lat], seq)], axis=1)
    u_o = u_o.reshape(no, (ncc + ncl) * SUBLANES, t * LANES).astype(BF16)
    y_g = s5_scan(_s5_relayout(u_o, split=True), m_mat, e_mat, r_mat, lam, ncc, ncc + ncl)
    y_o = _s5_relayout(y_g, split=False).reshape(no, ncc + ncl, SUBLANES, t, LANES)[:, :, :bsz]

    def from_chunks(y, length):
        return y.transpose(2, 1, 3, 0, 4).reshape(bsz * length, width)

    return jnp.concatenate([from_chunks(y_o[:, ncc:], seq), from_chunks(y_o[:, :ncc], n_ctx)], axis=0)


def _glu_kernel(y_ref, w_ref, b_ref, o_ref):
    gl = _gelu(y_ref[...].astype(F32))
    z = jnp.dot(gl.astype(BF16), w_ref[...], preferred_element_type=F32) + b_ref[...]
    o_ref[...] = (gl * jax.nn.sigmoid(z)).astype(o_ref.dtype)


def s5_glu(y, w, b, tm=512):
    m, n = y.shape
    return pl.pallas_call(
        _glu_kernel,
        grid=(m // tm,),
        in_specs=[pl.BlockSpec((tm, n), lambda i: (i, 0)),
                  pl.BlockSpec((n, n), lambda i: (0, 0)),
                  pl.BlockSpec((1, n), lambda i: (0, 0))],
        out_specs=pl.BlockSpec((tm, n), lambda i: (i, 0)),
        out_shape=jax.ShapeDtypeStruct((m, n), BF16),
        compiler_params=_cparams("arbitrary"),
        name="s5_glu",
    )(y, w, b.reshape(1, n))


_TOMB = 2.0 ** 100
_TOMB_GUARD = 2.0 ** 90


def _top_rows(s, n_take, break_ties):
    vals = []
    if not break_ties:
        for k in range(n_take):
            m = jnp.max(s, axis=0, keepdims=True)
            s = jnp.where(s == m, -_TOMB * (1.0 + k / 32.0), s)
            vals.append(m)
        rank = jnp.where(s <= -_TOMB, (s * (-1.0 / _TOMB) - 1.0) * 32.0, NOT_SELECTED)
        return vals, rank, s
    r = s.shape[0]
    rows = lax.broadcasted_iota(jnp.int32, s.shape, 0).astype(F32)
    rank = jnp.full(s.shape, NOT_SELECTED, F32)
    for k in range(n_take):
        m = jnp.max(s, axis=0, keepdims=True)
        idx = jnp.min(jnp.where(s == m, rows, float(r)), axis=0, keepdims=True)
        sel = rows == idx
        rank = jnp.where(sel, float(k), rank)
        s = jnp.where(sel, -jnp.inf, s)
        vals.append(m)
    return vals, rank, s


_CAND_WIDTHS = [PEER_TOPK // (i + 1) for i in range(PEER_TOPK)]
_CAND_STARTS = [sum(_CAND_WIDTHS[:i]) for i in range(PEER_TOPK + 1)]
_CAND_ROWS = -(-_CAND_STARTS[-1] // SUBLANES) * SUBLANES


def _route_head(s1, s2, grp, break_ties):
    v1, rank1, _ = _top_rows(s1, PEER_TOPK, break_ties)
    v2, rank2, _ = _top_rows(s2, PEER_TOPK, break_ties)
    v2m = jnp.concatenate(v2, axis=0)
    pad = jnp.full((_CAND_ROWS - _CAND_STARTS[-1], v2m.shape[1]), -jnp.inf, F32)
    cand = jnp.concatenate([v1[i] + v2m[0:_CAND_WIDTHS[i]] for i in range(PEER_TOPK)] + [pad], axis=0)
    cv, _, left = _top_rows(cand, PEER_TOPK, break_ties)
    picked = jnp.where(left != cand, 1.0, 0.0)
    picked = jnp.concatenate([picked, jnp.zeros((grp.shape[1] - _CAND_ROWS, picked.shape[1]), F32)], axis=0)
    cnt = jnp.dot(grp, picked, preferred_element_type=F32)
    z = jnp.zeros_like(cv[0])
    for k in range(PEER_TOPK):
        z = z + jnp.exp(cv[k] - cv[0])
    thr = jnp.zeros_like(s1)
    for i in range(PEER_TOPK):
        thr = jnp.where(rank1 == float(i), cnt[i:i + 1, :], thr)
    n_out = (jnp.sum(jnp.where(rank1 != NOT_SELECTED, 1.0, 0.0), axis=0, keepdims=True)
             + jnp.sum(jnp.where(rank2 != NOT_SELECTED, 1.0, 0.0), axis=0, keepdims=True)
             + jnp.sum(cnt, axis=0, keepdims=True))
    return thr, jnp.exp(s1 - v1[0]), rank2.astype(BF16), (jnp.exp(s2 - v2[0]) / z).astype(BF16), n_out


def _peer_route_kernel(q_ref, k1_ref, k2_ref, grp_ref, thr_ref, a_ref, r2_ref, b_ref):
    hp = lax.Precision.HIGHEST
    nt = (((1,), (1,)), ((), ()))
    for h in range(PEER_HEADS):
        qh = q_ref[:, h * 2 * PEER_HALF:(h + 1) * 2 * PEER_HALF]
        s1 = lax.dot_general(k1_ref[h], qh, nt, precision=hp, preferred_element_type=F32)
        s2 = lax.dot_general(k2_ref[h], qh, nt, precision=hp, preferred_element_type=F32)

        def write(res, h=h):
            thr_ref[h], a_ref[h], r2_ref[h], b_ref[h] = res[:4]

        fast = _route_head(s1, s2, grp_ref[...], break_ties=False)
        write(fast)

        low = jnp.minimum(jnp.min(s1), jnp.min(s2)) < -_TOMB_GUARD

        @pl.when((jnp.max(fast[4]) > 3.0 * PEER_TOPK) | low)
        def _(s1=s1, s2=s2, write=write):
            write(_route_head(s1, s2, grp_ref[...], break_ties=True))


def peer_route(q, keys, tt=256):
    t = q.shape[0]
    zeros = jnp.zeros((PEER_HEADS, PEER_NKEYS, PEER_HALF), F32)
    k1 = jnp.concatenate([keys[:, 0], zeros], axis=-1)
    k2 = jnp.concatenate([zeros, keys[:, 1]], axis=-1)
    row = jnp.arange(LANES)[None, :]
    grp = ((row >= jnp.array(_CAND_STARTS[:-1])[:, None]) & (row < jnp.array(_CAND_STARTS[1:])[:, None])).astype(F32)
    shp32 = jax.ShapeDtypeStruct((PEER_HEADS, PEER_NKEYS, t), F32)
    shp16 = jax.ShapeDtypeStruct((PEER_HEADS, PEER_NKEYS, t), BF16)
    ospec = pl.BlockSpec((PEER_HEADS, PEER_NKEYS, tt), lambda i: (0, 0, i))
    kspec = pl.BlockSpec((PEER_HEADS, PEER_NKEYS, 2 * PEER_HALF), lambda i: (0, 0, 0))
    return pl.pallas_call(
        _peer_route_kernel,
        grid=(t // tt,),
        in_specs=[pl.BlockSpec((tt, PEER_HEADS * 2 * PEER_HALF), lambda i: (i, 0)), kspec, kspec,
                  pl.BlockSpec((PEER_TOPK, LANES), lambda i: (0, 0))],
        out_specs=[ospec] * 4,
        out_shape=[shp32, shp32, shp16, shp16],
        compiler_params=_cparams("arbitrary"),
        name="peer_route",
    )(q, k1, k2, grp)


def _rows_bf16(row, n_rows):
    packed = jnp.broadcast_to(row, (2 * SUBLANES, row.shape[1])).astype(BF16)
    return jnp.concatenate([packed] * (n_rows // (2 * SUBLANES)), axis=0)


def _peer_dense_kernel(ft_ref, u_ref, vt_ref, thr_ref, a_ref, r2_ref, b_ref, gate_ref, res_ref, ng_ref, o_ref,
                       acc_scr, ge_scr, *, rows_per_tile, final_norm):
    j = pl.program_id(1)

    @pl.when(j == 0)
    def _():
        acc_scr[...] = jnp.zeros_like(acc_scr)
        ge_scr[...] = jnp.zeros_like(ge_scr)

    w_rows = []
    for r in range(rows_per_tile):
        w = None
        for h in range(PEER_HEADS):
            thr = _rows_bf16(thr_ref[h, r:r + 1, :], PEER_NKEYS)
            a = _rows_bf16(a_ref[h, r:r + 1, :], PEER_NKEYS)
            b = b_ref[h]
            term = jnp.where(r2_ref[h] < thr, b, jnp.zeros_like(b)) * a
            w = term if w is None else w + term
        w_rows.append(w * ge_scr[r * PEER_NKEYS:(r + 1) * PEER_NKEYS, :])
    wt = jnp.concatenate(w_rows, axis=0)
    acc_scr[...] += jnp.dot(vt_ref[...], wt, preferred_element_type=F32)
    act = jnp.dot(u_ref[...], ft_ref[...], preferred_element_type=F32)
    ge_scr[...] = _gelu(act).astype(BF16)

    @pl.when(j == pl.num_programs(1) - 1)
    def _():
        hn = res_ref[...] + gate_ref[...] * acc_scr[...].T
        if final_norm:
            ms = jnp.mean(hn * hn, axis=-1, keepdims=True)
            hn = hn * lax.rsqrt(ms + EPS) * ng_ref[...]
        o_ref[...] = hn


def peer_dense(ft, u, vt, layer, thr, a, r2, b, modr, gate_idx, res, mod_row, norm_g=None, tt=512,
               rows_per_tile=8):
    d, t = ft.shape
    e = u.shape[1]
    te = rows_per_tile * PEER_NKEYS
    ne = e // te

    def cur(j):
        return jnp.minimum(j, ne - 1)

    def prev(j):
        return jnp.maximum(j - 1, 0)

    rspec = pl.BlockSpec((PEER_HEADS, rows_per_tile, tt), lambda i, j: (0, prev(j), i))
    cspec = pl.BlockSpec((PEER_HEADS, PEER_NKEYS, tt), lambda i, j: (0, 0, i))
    return pl.pallas_call(
        functools.partial(_peer_dense_kernel, rows_per_tile=rows_per_tile, final_norm=norm_g is not None),
        grid=(t // tt, ne + 1),
        in_specs=[pl.BlockSpec((d, tt), lambda i, j: (0, i)),
                  pl.BlockSpec((None, te, d), lambda i, j: (layer, cur(j), 0)),
                  pl.BlockSpec((None, d, te), lambda i, j: (layer, 0, prev(j))),
                  rspec, rspec, cspec, cspec,
                  pl.BlockSpec((None, 1, d), lambda i, j: (mod_row(i * tt) * N_MOD + gate_idx, 0, 0)),
                  pl.BlockSpec((tt, d), lambda i, j: (i, 0)),
                  pl.BlockSpec((1, d), lambda i, j: (0, 0))],
        out_specs=pl.BlockSpec((tt, d), lambda i, j: (i, 0)),
        out_shape=jax.ShapeDtypeStruct((t, d), F32),
        scratch_shapes=[pltpu.VMEM((d, tt), F32), pltpu.VMEM((te, tt), BF16)],
        compiler_params=_cparams("arbitrary", "arbitrary"),
        name="peer_dense",
    )(ft, u, vt, thr, a, r2, b, modr, res, (jnp.ones((d,), F32) if norm_g is None else norm_g).reshape(1, d))


def peer_ffn(h, modr, mod_row, norm_g, wq, keys, u, vt, layer, tm, final_norm_g=None):
    q, ft = norm_mod_matmul(h, norm_g, modr, 3, 4, wq.astype(BF16), mod_row, tm=tm, tn=1024, emit_a=True)
    thr, a, r2, b = peer_route(q, keys)
    return peer_dense(ft, u, vt, layer, thr, a, r2, b, modr, 5, h, mod_row, norm_g=final_norm_g)


def _conv_kernel(x_ref, w_ref, b_ref, o_ref, pad_scr, *, length, row_tile):
    half = SSD_CONV // 2
    zeros = jnp.zeros((SUBLANES, x_ref.shape[1]), F32)
    pad_scr[0:SUBLANES, :] = zeros
    pad_scr[SUBLANES + length:2 * SUBLANES + length, :] = zeros
    pad_scr[SUBLANES:SUBLANES + length, :] = x_ref[...].astype(F32)
    for r0 in range(0, length, row_tile):
        acc = jnp.broadcast_to(b_ref[...], (row_tile, x_ref.shape[1]))
        for k in range(SSD_CONV):
            start = SUBLANES + r0 + k - half
            acc = acc + w_ref[k:k + 1, :] * pad_scr[start:start + row_tile, :]
        o_ref[r0:r0 + row_tile, :] = jax.nn.silu(acc).astype(o_ref.dtype)


def ssd_conv(p, col0, width, row0, n_seq, length, conv_w, conv_b, tc=256):
    row_tile = min(length, 512)
    return pl.pallas_call(
        functools.partial(_conv_kernel, length=length, row_tile=row_tile),
        grid=(n_seq, width // tc),
        in_specs=[pl.BlockSpec((length, tc), lambda s, j: (row0 // length + s, col0 // tc + j)),
                  pl.BlockSpec((SSD_CONV, tc), lambda s, j: (0, j)),
                  pl.BlockSpec((1, tc), lambda s, j: (0, j))],
        out_specs=pl.BlockSpec((length, tc), lambda s, j: (s, j)),
        out_shape=jax.ShapeDtypeStruct((n_seq * length, width), BF16),
        scratch_shapes=[pltpu.VMEM((length + 2 * SUBLANES, tc), F32)],
        compiler_params=_cparams("arbitrary", "arbitrary"),
        name="ssd_conv",
    )(p, conv_w, conv_b.reshape(1, width))


def _ssd_kernel(xl_ref, bl_ref, cl_ref, xx_ref, bx_ref, cx_ref, dt_ref, bias_ref, alog_ref, y_ref, st_scr, *,
                reverse, n_ctx_chunks):
    hp = lax.Precision.HIGHEST
    ch = SSD_CHUNK
    nh = dt_ref.shape[1]
    hpg = nh // SSD_GROUPS
    is_ctx = pl.program_id(1) < n_ctx_chunks

    def pick(ctx_ref, lat_ref, cols):
        return jnp.where(is_ctx, ctx_ref[:, cols], lat_ref[:, cols])

    @pl.when(pl.program_id(1) == 0)
    def _():
        st_scr[...] = jnp.zeros_like(st_scr)

    dt = jax.nn.softplus(dt_ref[...] + bias_ref[...])
    la = dt * (-jnp.exp(alog_ref[...]))
    row = lax.broadcasted_iota(jnp.int32, (ch, ch), 0)
    col = lax.broadcasted_iota(jnp.int32, (ch, ch), 1)
    keep = (col >= row) if reverse else (row >= col)
    tri = jnp.where(keep, 1.0, 0.0)
    cs = jnp.dot(tri, la, precision=hp, preferred_element_type=F32)
    both_t = jnp.concatenate([cs, dt], axis=1).T
    cs_t, dt_t = both_t[0:nh], both_t[nh:2 * nh]
    end = 0 if reverse else ch - 1
    tot_t = cs_t[:, end:end + 1]
    ws_t = jnp.exp(tot_t - cs_t) * dt_t
    dec_t = jnp.exp(tot_t)
    ecs = jnp.exp(cs)
    lo = lax.broadcasted_iota(jnp.int32, (ch, 2 * SSD_HEAD_DIM), 1) < SSD_HEAD_DIM
    nt = (((1,), (1,)), ((), ()))
    for g in range(SSD_GROUPS):
        gs = slice(g * SSD_STATE, (g + 1) * SSD_STATE)
        cg = pick(cx_ref, cl_ref, gs)
        bg = pick(bx_ref, bl_ref, gs)
        cb = lax.dot_general(cg, bg, nt, preferred_element_type=F32)
        bg_t = bg.astype(F32).T
        cg32 = cg.astype(F32)
        for pr in range(hpg // 2):
            lanes = slice((g * hpg + 2 * pr) * SSD_HEAD_DIM, (g * hpg + 2 * pr + 2) * SSD_HEAD_DIM)
            sl = slice(2 * pr * SSD_HEAD_DIM, (2 * pr + 2) * SSD_HEAD_DIM)
            xp = pick(xx_ref, xl_ref, lanes)
            st = st_scr[g, :, sl]
            rhs = jnp.concatenate([xp, st.astype(BF16)], axis=0)
            ys, news, decs = [], [], []
            for j in range(2):
                h = g * hpg + 2 * pr + j
                seg = cs[:, h:h + 1] - cs_t[h:h + 1, :]
                gmat = jnp.where(keep, jnp.exp(seg), 0.0) * cb * dt_t[h:h + 1, :]
                cmat = cg32 * ecs[:, h:h + 1]
                lhs = jnp.concatenate([gmat, cmat], axis=1).astype(BF16)
                ys.append(jnp.dot(lhs, rhs, preferred_element_type=F32))
                bw = (bg_t * ws_t[h:h + 1, :]).astype(BF16)
                news.append(jnp.dot(bw, xp, preferred_element_type=F32))
                decs.append(jnp.broadcast_to(dec_t[h:h + 1, :], (ch, 2 * SSD_HEAD_DIM)))
            y_ref[:, lanes] = jnp.where(lo, ys[0], ys[1]).astype(y_ref.dtype)
            st_scr[g, :, sl] = st * jnp.where(lo, decs[0], decs[1]) + jnp.where(lo, news[0], news[1])


def ssd_scan(xc_lat, xc_ctx, dt_raw, dt_bias, a_log, bsz, seq, n_ctx, reverse):
    nh = dt_raw.shape[1]
    inner = nh * SSD_HEAD_DIM
    gn = SSD_GROUPS * SSD_STATE
    nl, nc = seq // SSD_CHUNK, n_ctx // SSD_CHUNK
    lat_blocks = bsz * nl

    def lblk(b, s):
        sl = jnp.maximum(s - nc, 0)
        return b * nl + ((nl - 1 - sl) if reverse else sl)

    def cblk(b, s):
        sc = jnp.minimum(s, nc - 1)
        return b * nc + ((nc - 1 - sc) if reverse else sc)

    def blk(b, s):
        return jnp.where(s < nc, lat_blocks + cblk(b, s), lblk(b, s))

    def xbc_specs(fn):
        return [pl.BlockSpec((SSD_CHUNK, inner), lambda b, s: (fn(b, s), 0)),
                pl.BlockSpec((SSD_CHUNK, gn), lambda b, s: (fn(b, s), inner // gn)),
                pl.BlockSpec((SSD_CHUNK, gn), lambda b, s: (fn(b, s), inner // gn + 1))]

    return pl.pallas_call(
        functools.partial(_ssd_kernel, reverse=reverse, n_ctx_chunks=nc),
        grid=(bsz, nc + nl),
        in_specs=xbc_specs(lblk) + xbc_specs(cblk) + [
            pl.BlockSpec((SSD_CHUNK, nh), lambda b, s: (blk(b, s), 0)),
            pl.BlockSpec((1, nh), lambda b, s: (0, 0)),
            pl.BlockSpec((1, nh), lambda b, s: (0, 0))],
        out_specs=pl.BlockSpec((SSD_CHUNK, inner), lambda b, s: (lblk(b, s), 0)),
        out_shape=jax.ShapeDtypeStruct((bsz * seq, inner), BF16),
        scratch_shapes=[pltpu.VMEM((SSD_GROUPS, SSD_STATE, inner // SSD_GROUPS), F32)],
        compiler_params=_cparams("arbitrary", "arbitrary"),
        name="ssd_scan_bwd" if reverse else "ssd_scan_fwd",
    )(xc_lat, xc_lat, xc_lat, xc_ctx, xc_ctx, xc_ctx, dt_raw, dt_bias.reshape(1, nh), a_log.reshape(1, nh))


def _ssd_gate_kernel(yf_ref, yb_ref, x_ref, z_ref, d_ref, ng_ref, a_ref):
    y = (x_ref[...].astype(F32) * d_ref[...] + yf_ref[...].astype(F32) + yb_ref[...].astype(F32))
    gv = y * jax.nn.silu(z_ref[...].astype(F32))
    gw = gv.shape[1] // SSD_GROUPS
    for g in range(SSD_GROUPS):
        part = gv[:, g * gw:(g + 1) * gw]
        ms = jnp.mean(part * part, axis=-1, keepdims=True)
        a_ref[:, g * gw:(g + 1) * gw] = (part * lax.rsqrt(ms + EPS) * ng_ref[:, g * gw:(g + 1) * gw]).astype(BF16)


def ssd_output(yf, yb, xc, pz, d_exp, norm_g, w_out, modr, res, mod_row, tm=256):
    m, inner = yf.shape
    row_spec = pl.BlockSpec((tm, inner), lambda i: (i, 0))
    vec_spec = pl.BlockSpec((1, inner), lambda i: (0, 0))
    a = pl.pallas_call(
        _ssd_gate_kernel,
        grid=(m // tm,),
        in_specs=[row_spec, row_spec, row_spec, row_spec, vec_spec, vec_spec],
        out_specs=row_spec,
        out_shape=jax.ShapeDtypeStruct((m, inner), BF16),
        compiler_params=_cparams("arbitrary"),
        name="ssd_gate",
    )(yf, yb, xc, pz, d_exp.reshape(1, inner), norm_g.reshape(1, inner))
    return matmul_gated_residual([a], [w_out], modr, 2, res, mod_row, tm=512, tn=1024)


def _mod_rows(c, c_ctx, w, b, layer):
    bsz, d = c.shape
    cond = jnp.zeros((SUBLANES, d), F32).at[:bsz].set(c).at[bsz].set(c_ctx)
    mod = modulation(cond, w, b, layer)
    return mod.reshape(SUBLANES * N_MOD, 1, d)


def attn_s5_layer(h, bsz, seq, n_ctx, modr, mod_row, tm, norm1_g, w_in, sink, s5_params, glu_w, glu_b, w_out):
    n_lat = bsz * seq
    qw = ATTN_HEADS * HEAD_DIM
    kvw = ATTN_KV_HEADS * HEAD_DIM
    p = norm_mod_matmul(h, norm1_g, modr, 0, 1, w_in.astype(BF16), mod_row, tm=tm, tn=1280, out_dtype=BF16)
    attn_lat = attention_latent(p, sink, bsz, seq, n_ctx)
    attn_ctx = attention_context(p, sink, bsz, seq, n_ctx)
    attn = jnp.concatenate([attn_lat, attn_ctx], axis=0)
    y = s5_layer(p[:, qw + 2 * kvw:], bsz, seq, n_ctx, s5_params)
    s5o = s5_glu(y, glu_w.astype(BF16), glu_b)
    w_out = w_out.astype(BF16)
    return matmul_gated_residual([attn, s5o], [w_out[:qw], w_out[qw:]], modr, 2, h, mod_row, tm=tm, tn=1024)


def ssd_layer(h, bsz, seq, n_ctx, modr, mod_row, tm, norm1_g, w_in, conv_w, conv_b, dt_bias, a_log, d_skip,
              norm_g, w_out):
    n_lat = bsz * seq
    nh = a_log.shape[1]
    inner = nh * SSD_HEAD_DIM
    xbc_w = inner + 2 * SSD_GROUPS * SSD_STATE
    w_in = w_in.astype(BF16)
    pz = norm_mod_matmul(h, norm1_g, modr, 0, 1, w_in[:, :inner + xbc_w], mod_row, tm=tm, tn=2048,
                         out_dtype=BF16)
    dt_raw = norm_mod_matmul(h, norm1_g, modr, 0, 1, w_in[:, inner + xbc_w:], mod_row, tm=tm, tn=2 * nh)
    xc = ssd_conv(pz, inner, xbc_w, 0, bsz, seq, conv_w, conv_b)
    xc_ctx = ssd_conv(pz, inner, xbc_w, n_lat, bsz, n_ctx, conv_w, conv_b)
    yf = ssd_scan(xc, xc_ctx, dt_raw[:, :nh], dt_bias[0], a_log[0], bsz, seq, n_ctx, reverse=False)
    yb = ssd_scan(xc, xc_ctx, dt_raw[:, nh:], dt_bias[1], a_log[1], bsz, seq, n_ctx, reverse=True)
    d_exp = jnp.repeat(d_skip, SSD_HEAD_DIM)
    return ssd_output(yf, yb, xc, pz, d_exp, norm_g, w_out.astype(BF16), modr, h, mod_row)


def kernel(x, c, ctx, c_ctx, mod_w, mod_b, norm1_g, norm2_g, ab_w_in, attn_sink, s5_a_re, s5_a_im, s5_log_dt, s5_b_re, s5_b_im, s5_c_re, s5_c_im, s5_d, s5_glu_w, s5_glu_b, ab_w_out, ssd_w_in, ssd_conv_w, ssd_conv_b, ssd_dt_bias, ssd_a_log, ssd_d, ssd_norm_g, ssd_w_out, peer_wq, peer_keys, peer_u, peer_v, final_norm_g):
    bsz, seq, d = x.shape
    n_ctx = ctx.shape[1]
    n_lat = bsz * seq
    depth = mod_w.shape[0]
    tm = 512

    def mod_row(r):
        return jnp.minimum(r // seq, bsz)

    h = jnp.concatenate([x.reshape(n_lat, d), ctx.reshape(bsz * n_ctx, d)], axis=0)
    peer_u16 = peer_u.astype(BF16)
    peer_vt16 = peer_v.transpose(0, 2, 1).astype(BF16)
    for i in range(depth):
        last = i == depth - 1
        j = i // 2
        modr = _mod_rows(c, c_ctx, mod_w, mod_b, i)
        if i % 2 == 0:
            s5_params = (s5_a_re[j], s5_a_im[j], s5_log_dt[j], s5_b_re[j], s5_b_im[j], s5_c_re[j], s5_c_im[j],
                         s5_d[j])
            h = attn_s5_layer(h, bsz, seq, n_ctx, modr, mod_row, tm, norm1_g[i], ab_w_in[j], attn_sink[j],
                              s5_params, s5_glu_w[j], s5_glu_b[j], ab_w_out[j])
        else:
            if not last:
                raise NotImplementedError("an SSD layer that must also update the context stream")
            h = ssd_layer(h, bsz, seq, n_ctx, modr, mod_row, tm, norm1_g[i], ssd_w_in[j], ssd_conv_w[j],
                          ssd_conv_b[j], ssd_dt_bias[j], ssd_a_log[j], ssd_d[j], ssd_norm_g[j], ssd_w_out[j])
        h = peer_ffn(h, modr, mod_row, norm2_g[i], peer_wq[i], peer_keys[i], peer_u16, peer_vt16, i, tm,
                     final_norm_g=final_norm_g if last else None)
    return h[:n_lat].reshape(bsz, seq, d)
```
